```python
import jax, jax.numpy as jnp
from jax import lax
import numpy as np

D_MODEL = 1024
BATCH = 8
SEQ = 4096
DEPTH = 1

N_Q_HEADS = 16
N_KV_HEADS = 4
HEAD_DIM = 64
ROT_DIM = HEAD_DIM // 4
ROPE_THETA = 500000.0
WINDOW = 128
BLOCK = 128
CONV_DIM = D_MODEL
CONV_WIDTH = 31
D_FF = 4 * D_MODEL
N_BRANCH = 2
EPS = 1e-6
Q_W = N_Q_HEADS * HEAD_DIM
KV_W = N_KV_HEADS * HEAD_DIM
GATE_W = N_BRANCH * D_MODEL
IN_W = Q_W + 2 * KV_W + 2 * CONV_DIM + GATE_W

kernel_name = "hybrid_swa_sink_conformer_conv_gated_block"


def _rmsnorm(x, g):
    xf = x.astype(jnp.float32)
    y = xf * lax.rsqrt(jnp.mean(xf * xf, axis=-1, keepdims=True) + EPS)
    return (y * g.astype(jnp.float32)).astype(x.dtype)


def _layernorm(x, g, b):
    xf = x.astype(jnp.float32)
    mu = jnp.mean(xf, axis=-1, keepdims=True)
    var = jnp.mean(jnp.square(xf - mu), axis=-1, keepdims=True)
    y = (xf - mu) * lax.rsqrt(var + EPS)
    return (y * g.astype(jnp.float32) + b.astype(jnp.float32)).astype(x.dtype)


def _partial_rope(t, positions):
    half = ROT_DIM // 2
    inv_freq = 1.0 / (jnp.float32(ROPE_THETA) ** (jnp.arange(0, ROT_DIM, 2, dtype=jnp.float32) / ROT_DIM))
    ang = positions.astype(jnp.float32)[:, None] * inv_freq[None, :]
    cos = jnp.cos(ang)[None, :, None, :]
    sin = jnp.sin(ang)[None, :, None, :]
    tr = t[..., :ROT_DIM].astype(jnp.float32)
    t1, t2 = tr[..., :half], tr[..., half:]
    rot = jnp.concatenate([t1 * cos - t2 * sin, t2 * cos + t1 * sin], axis=-1)
    return jnp.concatenate([rot.astype(t.dtype), t[..., ROT_DIM:]], axis=-1)


def _band(t):
    b, s, h, d = t.shape
    tb = t.reshape(b, s // BLOCK, BLOCK, h, d)
    prev = jnp.pad(tb[:, :-1], ((0, 0), (1, 0), (0, 0), (0, 0), (0, 0)))
    return jnp.concatenate([prev, tb], axis=2)


def _sliding_window_sink_attention(q, k, v, sinks):
    b, s, _, d = q.shape
    nb = s // BLOCK
    grp = N_Q_HEADS // N_KV_HEADS
    qb = q.reshape(b, nb, BLOCK, N_KV_HEADS, grp, d)
    kb = _band(k)
    vb = _band(v)
    scores = jnp.einsum("bnqkgd,bnskd->bnkgqs", qb, kb).astype(jnp.float32) * (HEAD_DIM ** -0.5)
    i = jnp.arange(BLOCK)[:, None]
    j = jnp.arange(2 * BLOCK)[None, :]
    rel = i + BLOCK - j
    kpos = jnp.arange(nb)[:, None, None] * BLOCK - BLOCK + j[None]
    valid = (rel >= 0) & (rel < WINDOW) & (kpos >= 0)
    scores = jnp.where(valid[None, :, None, None], scores, -jnp.inf)
    sink = jnp.broadcast_to(
        sinks.astype(jnp.float32).reshape(N_KV_HEADS, grp)[None, None, :, :, None, None],
        scores.shape[:-1] + (1,))
    probs = jax.nn.softmax(jnp.concatenate([scores, sink], axis=-1), axis=-1)[..., :-1]
    out = jnp.einsum("bnkgqs,bnskd->bnqkgd", probs.astype(v.dtype), vb)
    return out.reshape(b, s, N_Q_HEADS * d)


def _conformer_conv(u, conv_w, conv_b, norm_g, norm_b, w_conv_out):
    a, gt = jnp.split(u, 2, axis=-1)
    glu = a * jax.nn.sigmoid(gt)
    y = lax.conv_general_dilated(
        glu, conv_w[:, None, :].astype(glu.dtype), window_strides=(1,),
        padding=[(CONV_WIDTH - 1, 0)], dimension_numbers=("NWC", "WIO", "NWC"),
        feature_group_count=CONV_DIM) + conv_b
    y = jax.nn.silu(_layernorm(y, norm_g, norm_b))
    return y @ w_conv_out


def setup_inputs(seed: int = 0) -> dict:
    key = jax.random.key(seed)
    ks = jax.random.split(key, 20)
    f32 = jnp.float32
    n = lambda k, shape, scale: jax.random.normal(k, shape, f32) * scale
    L = DEPTH
    return {
        "x": n(ks[0], (BATCH, SEQ, D_MODEL), 1.0),
        "norm1_g": 1.0 + n(ks[1], (L, D_MODEL), 0.02),
        "w_in": n(ks[2], (L, D_MODEL, IN_W), D_MODEL ** -0.5),
        "gate_b": n(ks[3], (L, GATE_W), 0.02),
        "q_norm_g": 1.0 + n(ks[4], (L, HEAD_DIM), 0.02),
        "k_norm_g": 1.0 + n(ks[5], (L, HEAD_DIM), 0.02),
        "sinks": n(ks[6], (L, N_Q_HEADS), 0.5),
        "w_attn_out": n(ks[7], (L, Q_W, D_MODEL), Q_W ** -0.5),
        "conv_w": n(ks[8], (L, CONV_WIDTH, CONV_DIM), CONV_WIDTH ** -0.5),
        "conv_b": n(ks[9], (L, CONV_DIM), 0.02),
        "conv_norm_g": 1.0 + n(ks[10], (L, CONV_DIM), 0.02),
        "conv_norm_b": n(ks[11], (L, CONV_DIM), 0.02),
        "w_conv_out": n(ks[12], (L, CONV_DIM, D_MODEL), CONV_DIM ** -0.5),
        "w_out": n(ks[13], (L, D_MODEL, D_MODEL), D_MODEL ** -0.5),
        "norm2_g": 1.0 + n(ks[14], (L, D_MODEL), 0.02),
        "w_up": n(ks[15], (L, D_MODEL, D_FF), D_MODEL ** -0.5),
        "w_down": n(ks[16], (L, D_FF, D_MODEL), D_FF ** -0.5),
    }


def reference(x, norm1_g, w_in, gate_b, q_norm_g, k_norm_g, sinks, w_attn_out,
              conv_w, conv_b, conv_norm_g, conv_norm_b, w_conv_out, w_out,
              norm2_g, w_up, w_down):
    b, s, _ = x.shape
    positions = jnp.arange(s, dtype=jnp.int32)
    splits = np.cumsum([Q_W, KV_W, KV_W, 2 * CONV_DIM]).tolist()
    for l in range(DEPTH):
        h = _rmsnorm(x, norm1_g[l])
        proj = h @ w_in[l]
        q, k, v, conv_in, gate_logits = jnp.split(proj, splits, axis=-1)
        q = q.reshape(b, s, N_Q_HEADS, HEAD_DIM)
        k = k.reshape(b, s, N_KV_HEADS, HEAD_DIM)
        v = v.reshape(b, s, N_KV_HEADS, HEAD_DIM)
        q = _partial_rope(_rmsnorm(q, q_norm_g[l]), positions)
        k = _partial_rope(_rmsnorm(k, k_norm_g[l]), positions)
        y_attn = _sliding_window_sink_attention(q, k, v, sinks[l]) @ w_attn_out[l]
        y_conv = _conformer_conv(conv_in, conv_w[l], conv_b[l], conv_norm_g[l],
                                 conv_norm_b[l], w_conv_out[l])
        gates = jax.nn.sigmoid(gate_logits + gate_b[l]).reshape(b, s, N_BRANCH, D_MODEL)
        merged = gates[:, :, 0] * y_attn + gates[:, :, 1] * y_conv
        x = x + merged @ w_out[l]
        hn = _rmsnorm(x, norm2_g[l])
        x = x + jnp.square(jax.nn.relu(hn @ w_up[l])) @ w_down[l]
    return x
```

```python
import functools
import math

import jax
import jax.numpy as jnp
from jax import lax
from jax.experimental import pallas as pl
from jax.experimental.pallas import tpu as pltpu

D_MODEL = 1024
N_Q_HEADS = 16
N_KV_HEADS = 4
GROUP = N_Q_HEADS // N_KV_HEADS
HEAD_DIM = 64
ROT_DIM = HEAD_DIM // 4
ROT_HALF = ROT_DIM // 2
ROPE_THETA = 500000.0
BLOCK = 128
CONV_WIDTH = 31
CONV_HALO = 32
D_FF = 4 * D_MODEL
EPS = 1e-6
Q_W = N_Q_HEADS * HEAD_DIM
KV_W = N_KV_HEADS * HEAD_DIM
QKV_W = Q_W + 2 * KV_W
QPAD = 2 * HEAD_DIM

SEQ_TILE = 256
MLP_TILE = 512
FF_CHUNK = 1024
ROW_CHUNK = 16
VMEM_LIMIT_BYTES = 56 * 1024 * 1024

_F32 = jnp.float32
_BF16 = jnp.bfloat16


def _rows(i, n):
    if isinstance(i, int):
        return pl.ds(i * n, n)
    return pl.ds(pl.multiple_of(i * n, n), n)


def _mixer_kernel(sinks_ref, x_ref, n1g_ref, wqkvt_ref, qkg_ref, wc_ref, wg_ref, gb_ref, wao_ref,
                  cw_ref, cb_ref, lng_ref, lnb_ref, wco_ref, wout_ref, o_ref,
                  h_s, qkv_s, qt_s, knat_s, vt_s, ot_s, ag_s, gbuf_s, y_s, ya_s, yc_s, m_s):
    T = SEQ_TILE
    t_idx = pl.program_id(1)
    first = t_idx == 0

    @pl.when(first)
    def _():
        qt_s[...] = jnp.zeros_like(qt_s)
        knat_s[0:BLOCK, :] = jnp.zeros((BLOCK, KV_W), _BF16)
        vt_s[:, 0:BLOCK] = jnp.zeros((KV_W, BLOCK), _BF16)
        gbuf_s[0:CONV_HALO, :] = jnp.zeros((CONV_HALO, D_MODEL), _F32)

    def norm_body(i, c):
        r = _rows(i, ROW_CHUNK)
        xv = x_ref[r, :]
        ms = jnp.mean(xv * xv, axis=-1, keepdims=True)
        h_s[r, :] = (xv * lax.rsqrt(ms + EPS) * n1g_ref[...]).astype(_BF16)
        return c
    lax.fori_loop(0, T // ROW_CHUNK, norm_body, 0)

    qkv_s[...] = lax.dot_general(wqkvt_ref[...], h_s[...], (((1,), (1,)), ((), ())),
                                 preferred_element_type=_F32)

    pos = (t_idx * T + lax.broadcasted_iota(jnp.int32, (ROT_HALF, T), 1)).astype(_F32)
    fidx = lax.broadcasted_iota(jnp.int32, (ROT_HALF, T), 0).astype(_F32)
    inv_freq = jnp.exp(fidx * (-2.0 * math.log(ROPE_THETA) / ROT_DIM))
    ang = pos * inv_freq
    cos = jnp.cos(ang)
    sin = jnp.sin(ang)

    def norm_rope(head, gain):
        xh = qkv_s[head * HEAD_DIM:(head + 1) * HEAD_DIM, :]
        ms = jnp.sum(xh * xh, axis=0, keepdims=True) * (1.0 / HEAD_DIM)
        yh = xh * lax.rsqrt(ms + EPS) * gain
        y1 = yh[0:ROT_HALF]
        y2 = yh[ROT_HALF:ROT_DIM]
        return jnp.concatenate([y1 * cos - y2 * sin, y2 * cos + y1 * sin, yh[ROT_DIM:]], axis=0)

    for h in range(N_Q_HEADS):
        off = h * QPAD + ((h // GROUP) % 2) * HEAD_DIM
        qt_s[off:off + HEAD_DIM, :] = norm_rope(h, qkg_ref[0]).astype(_BF16)
    kt = jnp.concatenate([norm_rope(N_Q_HEADS + j, qkg_ref[1]) for j in range(N_KV_HEADS)], axis=0)
    knat_s[BLOCK:BLOCK + T, :] = kt.T.astype(_BF16)
    vt_s[:, BLOCK:BLOCK + T] = qkv_s[Q_W + KV_W:QKV_W, :].astype(_BF16)

    kj = lax.broadcasted_iota(jnp.int32, (2 * BLOCK, BLOCK), 0)
    qi = lax.broadcasted_iota(jnp.int32, (2 * BLOCK, BLOCK), 1)
    in_cur = (kj >= BLOCK) & (kj - BLOCK <= qi)
    in_prev = (kj < BLOCK) & (kj > qi)
    neg_inf = jnp.full((2 * BLOCK, BLOCK), -jnp.inf, _F32)
    zero = jnp.zeros((2 * BLOCK, BLOCK), _F32)
    bias = jnp.where(in_cur | in_prev, zero, neg_inf)
    bias_first = jnp.where(first, jnp.where(in_cur, zero, neg_inf), bias)
    for blk in range(T // BLOCK):
        sbias = bias_first if blk == 0 else bias
        for kv in range(N_KV_HEADS):
            pair = kv // 2
            kb = knat_s[blk * BLOCK:(blk + 2) * BLOCK, pair * 2 * HEAD_DIM:(pair + 1) * 2 * HEAD_DIM]
            vb = vt_s[kv * HEAD_DIM:(kv + 1) * HEAD_DIM, blk * BLOCK:(blk + 2) * BLOCK]
            for g in range(GROUP):
                h = kv * GROUP + g
                qh = qt_s[h * QPAD:(h + 1) * QPAD, blk * BLOCK:(blk + 1) * BLOCK]
                s = jnp.dot(kb, qh, preferred_element_type=_F32)
                s = s + sbias
                sink = sinks_ref[h]
                m = jnp.maximum(jnp.max(s, axis=0, keepdims=True), sink)
                p = jnp.exp(s - m)
                l = jnp.sum(p, axis=0, keepdims=True) + jnp.exp(sink - m)
                o = jnp.dot(vb, p.astype(_BF16), preferred_element_type=_F32)
                ot_s[h * HEAD_DIM:(h + 1) * HEAD_DIM, blk * BLOCK:(blk + 1) * BLOCK] = (
                    o * (1.0 / l)).astype(_BF16)

    ag_s[...] = jnp.dot(h_s[...], wc_ref[...], preferred_element_type=_F32)

    def glu_body(i, c):
        r = _rows(i, ROW_CHUNK)
        a = ag_s[r, 0:D_MODEL]
        gt = ag_s[r, D_MODEL:2 * D_MODEL]
        gbuf_s[pl.ds(pl.multiple_of(CONV_HALO + i * ROW_CHUNK, ROW_CHUNK), ROW_CHUNK), :] = (
            a * jax.nn.sigmoid(gt))
        return c
    lax.fori_loop(0, T // ROW_CHUNK, glu_body, 0)

    def conv_body(i, c):
        base = i * ROW_CHUNK + (CONV_HALO - (CONV_WIDTH - 1))
        acc = jnp.broadcast_to(cb_ref[...], (ROW_CHUNK, D_MODEL))
        for k in range(CONV_WIDTH):
            acc = acc + cw_ref[k:k + 1, :] * gbuf_s[pl.ds(base + k, ROW_CHUNK), :]
        mu = jnp.mean(acc, axis=-1, keepdims=True)
        d = acc - mu
        var = jnp.mean(d * d, axis=-1, keepdims=True)
        yn = d * lax.rsqrt(var + EPS) * lng_ref[...] + lnb_ref[...]
        y_s[_rows(i, ROW_CHUNK), :] = (yn * jax.nn.sigmoid(yn)).astype(_BF16)
        return c
    for i in range(T // ROW_CHUNK):
        conv_body(i, 0)

    yc_s[...] = jnp.dot(y_s[...], wco_ref[...], preferred_element_type=_F32)
    ya_s[...] = lax.dot_general(ot_s[...], wao_ref[...], (((0,), (0,)), ((), ())),
                                preferred_element_type=_F32)

    ag_s[...] = jnp.dot(h_s[...], wg_ref[...], preferred_element_type=_F32)

    def merge_body(i, c):
        r = _rows(i, ROW_CHUNK)
        g0 = jax.nn.sigmoid(ag_s[r, 0:D_MODEL] + gb_ref[:, 0:D_MODEL])
        g1 = jax.nn.sigmoid(ag_s[r, D_MODEL:2 * D_MODEL] + gb_ref[:, D_MODEL:2 * D_MODEL])
        m_s[r, :] = (g0 * ya_s[r, :] + g1 * yc_s[r, :]).astype(_BF16)
        return c
    lax.fori_loop(0, T // ROW_CHUNK, merge_body, 0)

    o_ref[...] = x_ref[...] + jnp.dot(m_s[...], wout_ref[...], preferred_element_type=_F32)

    knat_s[0:BLOCK, :] = knat_s[T:T + BLOCK, :]
    vt_s[:, 0:BLOCK] = vt_s[:, T:T + BLOCK]
    gbuf_s[0:CONV_HALO, :] = gbuf_s[T:T + CONV_HALO, :]


def _mlp_kernel(x_ref, n2g_ref, wup_ref, wdn_ref, o_ref, hn_s):
    T = MLP_TILE

    def norm_body(i, c):
        r = _rows(i, ROW_CHUNK)
        xv = x_ref[r, :]
        ms = jnp.mean(xv * xv, axis=-1, keepdims=True)
        hn_s[r, :] = (xv * lax.rsqrt(ms + EPS) * n2g_ref[...]).astype(_BF16)
        return c
    lax.fori_loop(0, T // ROW_CHUNK, norm_body, 0)

    acc = x_ref[...]
    for c in range(D_FF // FF_CHUNK):
        u = jnp.dot(hn_s[...], wup_ref[:, c * FF_CHUNK:(c + 1) * FF_CHUNK], preferred_element_type=_F32)
        u = jnp.maximum(u, 0.0)
        acc = acc + jnp.dot((u * u).astype(_BF16), wdn_ref[c * FF_CHUNK:(c + 1) * FF_CHUNK, :],
                            preferred_element_type=_F32)
    o_ref[...] = acc


def _const_spec(shape):
    zeros = (0,) * len(shape)
    return pl.BlockSpec(shape, lambda *_: zeros, pipeline_mode=pl.Buffered(1))


def _mixer_call(x, sinks, n1g, wqkvt, qkg, wc, wg, gb, wao, cw, cb, lng, lnb, wco, wout):
    B, S, D = x.shape
    T = SEQ_TILE
    const_args = (n1g, wqkvt, qkg, wc, wg, gb, wao, cw, cb, lng, lnb, wco, wout)
    return pl.pallas_call(
        _mixer_kernel,
        out_shape=jax.ShapeDtypeStruct((B, S, D), _F32),
        grid=(B, S // T),
        in_specs=[pl.BlockSpec(memory_space=pltpu.SMEM),
                  pl.BlockSpec((None, T, D), lambda b, t: (b, t, 0))]
                 + [_const_spec(a.shape) for a in const_args],
        out_specs=pl.BlockSpec((None, T, D), lambda b, t: (b, t, 0)),
        scratch_shapes=[
            pltpu.VMEM((T, D), _BF16),
            pltpu.VMEM((QKV_W, T), _F32),
            pltpu.VMEM((N_Q_HEADS * QPAD, T), _BF16),
            pltpu.VMEM((BLOCK + T, KV_W), _BF16),
            pltpu.VMEM((KV_W, BLOCK + T), _BF16),
            pltpu.VMEM((Q_W, T), _BF16),
            pltpu.VMEM((T, 2 * D), _F32),
            pltpu.VMEM((CONV_HALO + T, D), _F32),
            pltpu.VMEM((T, D), _BF16),
            pltpu.VMEM((T, D), _F32),
            pltpu.VMEM((T, D), _F32),
            pltpu.VMEM((T, D), _BF16),
        ],
        compiler_params=pltpu.CompilerParams(
            dimension_semantics=("arbitrary", "arbitrary"),
            vmem_limit_bytes=VMEM_LIMIT_BYTES),
        name="mixer",
    )(sinks, x, *const_args)


def _mlp_call(x2d, n2g, wup, wdn):
    N, D = x2d.shape
    T = MLP_TILE
    return pl.pallas_call(
        _mlp_kernel,
        out_shape=jax.ShapeDtypeStruct((N, D), _F32),
        grid=(N // T,),
        in_specs=[pl.BlockSpec((T, D), lambda i: (i, 0)),
                  _const_spec(n2g.shape), _const_spec(wup.shape), _const_spec(wdn.shape)],
        out_specs=pl.BlockSpec((T, D), lambda i: (i, 0)),
        scratch_shapes=[pltpu.VMEM((T, D), _BF16)],
        compiler_params=pltpu.CompilerParams(
            dimension_semantics=("arbitrary",),
            vmem_limit_bytes=VMEM_LIMIT_BYTES),
        name="mlp",
    )(x2d, n2g, wup, wdn)


def kernel(x, norm1_g, w_in, gate_b, q_norm_g, k_norm_g, sinks, w_attn_out, conv_w, conv_b,
           conv_norm_g, conv_norm_b, w_conv_out, w_out, norm2_g, w_up, w_down):
    B, S, D = x.shape
    depth = w_in.shape[0]
    assert D == D_MODEL and S % SEQ_TILE == 0 and (B * S) % MLP_TILE == 0
    c0, c1 = QKV_W, QKV_W + 2 * D_MODEL
    for l in range(depth):
        wqkvt = w_in[l, :, :c0].T.astype(_BF16)
        wc = w_in[l, :, c0:c1].astype(_BF16)
        wg = w_in[l, :, c1:].astype(_BF16)
        qkg = jnp.stack([
            jnp.broadcast_to((q_norm_g[l] * HEAD_DIM ** -0.5)[:, None], (HEAD_DIM, SEQ_TILE)),
            jnp.broadcast_to(k_norm_g[l][:, None], (HEAD_DIM, SEQ_TILE))])
        x = _mixer_call(
            x, sinks[l], norm1_g[l][None], wqkvt, qkg, wc, wg, gate_b[l][None],
            w_attn_out[l].astype(_BF16), conv_w[l], conv_b[l][None], conv_norm_g[l][None],
            conv_norm_b[l][None], w_conv_out[l].astype(_BF16), w_out[l].astype(_BF16))
        x = _mlp_call(x.reshape(B * S, D), norm2_g[l][None], w_up[l].astype(_BF16),
                      w_down[l].astype(_BF16)).reshape(B, S, D)
    return x
```

```python
import functools
import math

import jax
import jax.numpy as jnp
from jax import lax
from jax.experimental import pallas as pl
from jax.experimental.pallas import tpu as pltpu

D_MODEL = 1024
N_Q_HEADS = 16
N_KV_HEADS = 4
GROUP = N_Q_HEADS // N_KV_HEADS
HEAD_DIM = 64
ROT_DIM = HEAD_DIM // 4
ROT_HALF = ROT_DIM // 2
ROPE_THETA = 500000.0
BLOCK = 128
CONV_WIDTH = 31
CONV_HALO = 32
D_FF = 4 * D_MODEL
EPS = 1e-6
Q_W = N_Q_HEADS * HEAD_DIM
KV_W = N_KV_HEADS * HEAD_DIM
QKV_W = Q_W + 2 * KV_W
QPAD = 2 * HEAD_DIM

LANES = 128
SUBLANES = 8
N_COL = D_MODEL // LANES
CONV_STRIDE = 4
CONV_GROUP = CONV_STRIDE * SUBLANES
CONV_COLS = 2

SEQ_TILE = 256
MLP_TILE = 512
FF_CHUNK = 1024
ROW_CHUNK = 16
ROW_UNROLL = 4
SOFTMAX_UNROLL = 2
VMEM_LIMIT_BYTES = 56 * 1024 * 1024

_F32 = jnp.float32
_BF16 = jnp.bfloat16


def _rows(i, n):
    if isinstance(i, int):
        return pl.ds(i * n, n)
    return pl.ds(pl.multiple_of(i * n, n), n)


def _mixer_kernel(sinks_ref, x_ref, n1g_ref, wqkvt_ref, qkg_ref, wc_ref, wg_ref, gb_ref, wao_ref,
                  cwb_ref, cb_ref, lng_ref, lnb_ref, wco_ref, wout_ref, o_ref,
                  h_s, qkv_s, qt_s, knat_s, vt_s, ot_s, ag_s, gbuf_s, cv_s, y_s, ya_s, yc_s, m_s,
                  s_s, p_s, linv_s):
    T = SEQ_TILE
    t_idx = pl.program_id(1)
    first = t_idx == 0

    @pl.when(first)
    def _():
        qt_s[...] = jnp.zeros_like(qt_s)
        knat_s[0:BLOCK, :] = jnp.zeros((BLOCK, KV_W), _BF16)
        vt_s[:, 0:BLOCK] = jnp.zeros((KV_W, BLOCK), _BF16)
        gbuf_s[:, 0:CONV_HALO, :] = jnp.zeros((N_COL, CONV_HALO, LANES), _F32)

    def norm_body(i, c):
        r = _rows(i, ROW_CHUNK)
        xv = x_ref[r, :]
        ms = jnp.mean(xv * xv, axis=-1, keepdims=True)
        h_s[r, :] = (xv * lax.rsqrt(ms + EPS) * n1g_ref[...]).astype(_BF16)
        return c
    lax.fori_loop(0, T // ROW_CHUNK, norm_body, 0, unroll=ROW_UNROLL)

    qkv_s[...] = lax.dot_general(wqkvt_ref[...], h_s[...], (((1,), (1,)), ((), ())),
                                 preferred_element_type=_F32)

    pos = (t_idx * T + lax.broadcasted_iota(jnp.int32, (ROT_HALF, T), 1)).astype(_F32)
    fidx = lax.broadcasted_iota(jnp.int32, (ROT_HALF, T), 0).astype(_F32)
    inv_freq = jnp.exp(fidx * (-2.0 * math.log(ROPE_THETA) / ROT_DIM))
    ang = pos * inv_freq
    cos = jnp.cos(ang)
    sin = jnp.sin(ang)

    def norm_rope(head, gain):
        xh = qkv_s[head * HEAD_DIM:(head + 1) * HEAD_DIM, :]
        ms = jnp.sum(xh * xh, axis=0, keepdims=True) * (1.0 / HEAD_DIM)
        yh = xh * lax.rsqrt(ms + EPS) * gain
        y1 = yh[0:ROT_HALF]
        y2 = yh[ROT_HALF:ROT_DIM]
        return jnp.concatenate([y1 * cos - y2 * sin, y2 * cos + y1 * sin, yh[ROT_DIM:]], axis=0)

    for h in range(N_Q_HEADS):
        off = h * QPAD + ((h // GROUP) % 2) * HEAD_DIM
        qt_s[off:off + HEAD_DIM, :] = norm_rope(h, qkg_ref[0]).astype(_BF16)
    kt = jnp.concatenate([norm_rope(N_Q_HEADS + j, qkg_ref[1]) for j in range(N_KV_HEADS)], axis=0)
    knat_s[BLOCK:BLOCK + T, :] = kt.T.astype(_BF16)
    vt_s[:, BLOCK:BLOCK + T] = qkv_s[Q_W + KV_W:QKV_W, :].astype(_BF16)

    kj = lax.broadcasted_iota(jnp.int32, (2 * BLOCK, BLOCK), 0)
    qi = lax.broadcasted_iota(jnp.int32, (2 * BLOCK, BLOCK), 1)
    in_cur = (kj >= BLOCK) & (kj - BLOCK <= qi)
    in_prev = (kj < BLOCK) & (kj > qi)
    neg_inf = jnp.full((2 * BLOCK, BLOCK), -jnp.inf, _F32)
    zero = jnp.zeros((2 * BLOCK, BLOCK), _F32)
    bias = jnp.where(in_cur | in_prev, zero, neg_inf)
    bias_first = jnp.where(first, jnp.where(in_cur, zero, neg_inf), bias)
    for blk in range(T // BLOCK):
        sbias = bias_first if blk == 0 else bias
        qcols = slice(blk * BLOCK, (blk + 1) * BLOCK)
        for h in range(N_Q_HEADS):
            pair = (h // GROUP) // 2
            kb = knat_s[blk * BLOCK:(blk + 2) * BLOCK, pair * 2 * HEAD_DIM:(pair + 1) * 2 * HEAD_DIM]
            s_s[h] = jnp.dot(kb, qt_s[h * QPAD:(h + 1) * QPAD, qcols],
                             preferred_element_type=_F32) + sbias

        def softmax_body(h, c):
            s = s_s[h]
            sink = sinks_ref[h]
            m = jnp.maximum(jnp.max(s, axis=0, keepdims=True), sink)
            p = jnp.exp(s - m)
            l = jnp.sum(p, axis=0, keepdims=True) + jnp.exp(sink - m)
            p_s[h] = p.astype(_BF16)
            linv_s[h] = jnp.broadcast_to(1.0 / l, (SUBLANES, BLOCK))
            return c
        lax.fori_loop(0, N_Q_HEADS, softmax_body, 0, unroll=SOFTMAX_UNROLL)

        for h in range(N_Q_HEADS):
            kv = h // GROUP
            vb = vt_s[kv * HEAD_DIM:(kv + 1) * HEAD_DIM, blk * BLOCK:(blk + 2) * BLOCK]
            o = jnp.dot(vb, p_s[h], preferred_element_type=_F32)
            ot_s[h * HEAD_DIM:(h + 1) * HEAD_DIM, qcols] = (o * linv_s[h, 0:1, :]).astype(_BF16)

    ag_s[...] = jnp.dot(h_s[...], wc_ref[...], preferred_element_type=_F32)

    def glu_body(i, c):
        r = _rows(i, ROW_CHUNK)
        glu = ag_s[r, 0:D_MODEL] * jax.nn.sigmoid(ag_s[r, D_MODEL:2 * D_MODEL])
        dst = pl.ds(pl.multiple_of(CONV_HALO + i * ROW_CHUNK, ROW_CHUNK), ROW_CHUNK)
        for col in range(N_COL):
            gbuf_s[col, dst, :] = glu[:, col * LANES:(col + 1) * LANES]
        return c
    lax.fori_loop(0, T // ROW_CHUNK, glu_body, 0, unroll=ROW_UNROLL)

    def conv_body(gi, c):
        t0 = pl.multiple_of(gi * CONV_GROUP, CONV_GROUP)
        base = t0 + (CONV_HALO - (CONV_WIDTH - 1))
        for c0 in range(0, N_COL, CONV_COLS):
            cols = range(c0, c0 + CONV_COLS)
            acc = {}
            xs = {}
            for k in range(CONV_WIDTH):
                for col in cols:
                    wk = cwb_ref[k, :, col * LANES:(col + 1) * LANES]
                    for r in range(CONV_STRIDE):
                        j = k + r
                        if (j, col) not in xs:
                            xs[j, col] = gbuf_s[col, pl.ds(base + j, SUBLANES, stride=CONV_STRIDE), :]
                        term = wk * xs[j, col]
                        acc[r, col] = term if k == 0 else acc[r, col] + term
            for r in range(CONV_STRIDE):
                for col in cols:
                    cv_s[col, pl.ds(t0 + r, SUBLANES, stride=CONV_STRIDE), :] = (
                        acc[r, col] + cb_ref[:, col * LANES:(col + 1) * LANES])
        return c
    lax.fori_loop(0, T // CONV_GROUP, conv_body, 0)

    def ln_body(i, c):
        r = _rows(i, ROW_CHUNK)
        row = jnp.concatenate([cv_s[col, r, :] for col in range(N_COL)], axis=1)
        mu = jnp.mean(row, axis=-1, keepdims=True)
        d = row - mu
        var = jnp.mean(d * d, axis=-1, keepdims=True)
        yn = d * lax.rsqrt(var + EPS) * lng_ref[...] + lnb_ref[...]
        y_s[r, :] = (yn * jax.nn.sigmoid(yn)).astype(_BF16)
        return c
    lax.fori_loop(0, T // ROW_CHUNK, ln_body, 0, unroll=ROW_UNROLL)

    yc_s[...] = jnp.dot(y_s[...], wco_ref[...], preferred_element_type=_F32)
    ya_s[...] = lax.dot_general(ot_s[...], wao_ref[...], (((0,), (0,)), ((), ())),
                                preferred_element_type=_F32)

    ag_s[...] = jnp.dot(h_s[...], wg_ref[...], preferred_element_type=_F32)

    def merge_body(i, c):
        r = _rows(i, ROW_CHUNK)
        g0 = jax.nn.sigmoid(ag_s[r, 0:D_MODEL] + gb_ref[:, 0:D_MODEL])
        g1 = jax.nn.sigmoid(ag_s[r, D_MODEL:2 * D_MODEL] + gb_ref[:, D_MODEL:2 * D_MODEL])
        m_s[r, :] = (g0 * ya_s[r, :] + g1 * yc_s[r, :]).astype(_BF16)
        return c
    lax.fori_loop(0, T // ROW_CHUNK, merge_body, 0, unroll=ROW_UNROLL)

    o_ref[...] = x_ref[...] + jnp.dot(m_s[...], wout_ref[...], preferred_element_type=_F32)

    knat_s[0:BLOCK, :] = knat_s[T:T + BLOCK, :]
    vt_s[:, 0:BLOCK] = vt_s[:, T:T + BLOCK]
    gbuf_s[:, 0:CONV_HALO, :] = gbuf_s[:, T:T + CONV_HALO, :]


def _mlp_kernel(x_ref, n2g_ref, wup_ref, wdn_ref, o_ref, hn_s):
    T = MLP_TILE

    def norm_body(i, c):
        r = _rows(i, ROW_CHUNK)
        xv = x_ref[r, :]
        ms = jnp.mean(xv * xv, axis=-1, keepdims=True)
        hn_s[r, :] = (xv * lax.rsqrt(ms + EPS) * n2g_ref[...]).astype(_BF16)
        return c
    lax.fori_loop(0, T // ROW_CHUNK, norm_body, 0, unroll=ROW_UNROLL)

    acc = x_ref[...]
    for c in range(D_FF // FF_CHUNK):
        u = jnp.dot(hn_s[...], wup_ref[:, c * FF_CHUNK:(c + 1) * FF_CHUNK], preferred_element_type=_F32)
        u = jnp.maximum(u, 0.0)
        acc = acc + jnp.dot((u * u).astype(_BF16), wdn_ref[c * FF_CHUNK:(c + 1) * FF_CHUNK, :],
                            preferred_element_type=_F32)
    o_ref[...] = acc


def _const_spec(shape):
    zeros = (0,) * len(shape)
    return pl.BlockSpec(shape, lambda *_: zeros, pipeline_mode=pl.Buffered(1))


def _mixer_call(x, sinks, n1g, wqkvt, qkg, wc, wg, gb, wao, cw, cb, lng, lnb, wco, wout):
    B, S, D = x.shape
    T = SEQ_TILE
    const_args = (n1g, wqkvt, qkg, wc, wg, gb, wao, cw, cb, lng, lnb, wco, wout)
    return pl.pallas_call(
        _mixer_kernel,
        out_shape=jax.ShapeDtypeStruct((B, S, D), _F32),
        grid=(B, S // T),
        in_specs=[pl.BlockSpec(memory_space=pltpu.SMEM),
                  pl.BlockSpec((None, T, D), lambda b, t: (b, t, 0))]
                 + [_const_spec(a.shape) for a in const_args],
        out_specs=pl.BlockSpec((None, T, D), lambda b, t: (b, t, 0)),
        scratch_shapes=[
            pltpu.VMEM((T, D), _BF16),
            pltpu.VMEM((QKV_W, T), _F32),
            pltpu.VMEM((N_Q_HEADS * QPAD, T), _BF16),
            pltpu.VMEM((BLOCK + T, KV_W), _BF16),
            pltpu.VMEM((KV_W, BLOCK + T), _BF16),
            pltpu.VMEM((Q_W, T), _BF16),
            pltpu.VMEM((T, 2 * D), _F32),
            pltpu.VMEM((N_COL, CONV_HALO + T, LANES), _F32),
            pltpu.VMEM((N_COL, T, LANES), _F32),
            pltpu.VMEM((T, D), _BF16),
            pltpu.VMEM((T, D), _F32),
            pltpu.VMEM((T, D), _F32),
            pltpu.VMEM((T, D), _BF16),
            pltpu.VMEM((N_Q_HEADS, 2 * BLOCK, BLOCK), _F32),
            pltpu.VMEM((N_Q_HEADS, 2 * BLOCK, BLOCK), _BF16),
            pltpu.VMEM((N_Q_HEADS, SUBLANES, BLOCK), _F32),
        ],
        compiler_params=pltpu.CompilerParams(
            dimension_semantics=("arbitrary", "arbitrary"),
            vmem_limit_bytes=VMEM_LIMIT_BYTES),
        name="mixer",
    )(sinks, x, *const_args)


def _mlp_call(x2d, n2g, wup, wdn):
    N, D = x2d.shape
    T = MLP_TILE
    return pl.pallas_call(
        _mlp_kernel,
        out_shape=jax.ShapeDtypeStruct((N, D), _F32),
        grid=(N // T,),
        in_specs=[pl.BlockSpec((T, D), lambda i: (i, 0)),
                  _const_spec(n2g.shape), _const_spec(wup.shape), _const_spec(wdn.shape)],
        out_specs=pl.BlockSpec((T, D), lambda i: (i, 0)),
        scratch_shapes=[pltpu.VMEM((T, D), _BF16)],
        compiler_params=pltpu.CompilerParams(
            dimension_semantics=("arbitrary",),
            vmem_limit_bytes=VMEM_LIMIT_BYTES),
        name="mlp",
    )(x2d, n2g, wup, wdn)


def kernel(x, norm1_g, w_in, gate_b, q_norm_g, k_norm_g, sinks, w_attn_out, conv_w, conv_b,
           conv_norm_g, conv_norm_b, w_conv_out, w_out, norm2_g, w_up, w_down):
    B, S, D = x.shape
    depth = w_in.shape[0]
    assert D == D_MODEL and S % SEQ_TILE == 0 and (B * S) % MLP_TILE == 0
    c0, c1 = QKV_W, QKV_W + 2 * D_MODEL
    for l in range(depth):
        wqkvt = w_in[l, :, :c0].T.astype(_BF16)
        wc = w_in[l, :, c0:c1].astype(_BF16)
        wg = w_in[l, :, c1:].astype(_BF16)
        qkg = jnp.stack([
            jnp.broadcast_to((q_norm_g[l] * HEAD_DIM ** -0.5)[:, None], (HEAD_DIM, SEQ_TILE)),
            jnp.broadcast_to(k_norm_g[l][:, None], (HEAD_DIM, SEQ_TILE))])
        x = _mixer_call(
            x, sinks[l], norm1_g[l][None], wqkvt, qkg, wc, wg, gate_b[l][None],
            w_attn_out[l].astype(_BF16),
            jnp.broadcast_to(conv_w[l][:, None, :], (CONV_WIDTH, SUBLANES, D_MODEL)),
            conv_b[l][None], conv_norm_g[l][None],
            conv_norm_b[l][None], w_conv_out[l].astype(_BF16), w_out[l].astype(_BF16))
        x = _mlp_call(x.reshape(B * S, D), norm2_g[l][None], w_up[l].astype(_BF16),
                      w_down[l].astype(_BF16)).reshape(B, S, D)
    return x
```

```python
import math

import jax
import jax.numpy as jnp
from jax import lax
from jax.experimental import pallas as pl
from jax.experimental.pallas import tpu as pltpu

D_MODEL = 1024
N_Q_HEADS = 16
N_KV_HEADS = 4
GROUP = N_Q_HEADS // N_KV_HEADS
HEAD_DIM = 64
ROT_DIM = HEAD_DIM // 4
ROT_HALF = ROT_DIM // 2
ROPE_THETA = 500000.0
LOG2_E = math.log2(math.e)
BLOCK = 128
CONV_WIDTH = 31
CONV_HALO = 32
D_FF = 4 * D_MODEL
EPS = 1e-6
Q_W = N_Q_HEADS * HEAD_DIM
KV_W = N_KV_HEADS * HEAD_DIM
QKV_W = Q_W + 2 * KV_W
QPAD = 2 * HEAD_DIM

LANES = 128
SUBLANES = 8
N_COL = D_MODEL // LANES
CONV_STRIDE = 4
CONV_GROUP = CONV_STRIDE * SUBLANES
CONV_COLS = 2

SEQ_TILE = 256
MLP_TILE = 512
FF_CHUNK = 1024
ROW_CHUNK = 16
DOT_ROWS = 128
VMEM_LIMIT_BYTES = 56 * 1024 * 1024

_F32 = jnp.float32
_BF16 = jnp.bfloat16


def _rows(i, n):
    return pl.ds(i * n, n)


def _mixer_kernel(sinks_ref, x_ref, n1g_ref, wqkvt_ref, qkg_ref, wc_ref, wg_ref, gb_ref, wao_ref,
                  cwb_ref, cb_ref, lng_ref, lnb_ref, wco_ref, wout_ref, o_ref,
                  h_s, qkv_s, qt_s, knat_s, vt_s, ot_s, ag_s, gbuf_s, cv_s, y_s, ya_s, yc_s, m_s,
                  s_s, p_s, linv_s, gl_s):
    T = SEQ_TILE
    n_blk = T // BLOCK
    t_idx = pl.program_id(1)
    first = t_idx == 0

    @pl.when(first)
    def _():
        qt_s[...] = jnp.zeros_like(qt_s)
        knat_s[0:BLOCK, :] = jnp.zeros((BLOCK, KV_W), _BF16)
        vt_s[:, 0:BLOCK] = jnp.zeros((KV_W, BLOCK), _BF16)
        gbuf_s[:, 0:CONV_HALO, :] = jnp.zeros((N_COL, CONV_HALO, LANES), _F32)

    for i in range(T // ROW_CHUNK):
        r = _rows(i, ROW_CHUNK)
        xv = x_ref[r, :]
        ms = jnp.mean(xv * xv, axis=-1, keepdims=True)
        h_s[r, :] = (xv * lax.rsqrt(ms + EPS) * n1g_ref[...]).astype(_BF16)

    for c in range(T // DOT_ROWS):
        r = _rows(c, DOT_ROWS)
        ag_s[r, :] = jnp.dot(h_s[r, :], wc_ref[...], preferred_element_type=_F32)
    qkv_s[...] = lax.dot_general(wqkvt_ref[...], h_s[...], (((1,), (1,)), ((), ())),
                                 preferred_element_type=_F32)
    gl_s[...] = jnp.dot(h_s[...], wg_ref[...], preferred_element_type=_F32)

    for i in range(T // ROW_CHUNK):
        r = _rows(i, ROW_CHUNK)
        glu = ag_s[r, 0:D_MODEL] * jax.nn.sigmoid(ag_s[r, D_MODEL:2 * D_MODEL])
        for col in range(N_COL):
            gbuf_s[col, pl.ds(CONV_HALO + i * ROW_CHUNK, ROW_CHUNK), :] = glu[:, col * LANES:(col + 1) * LANES]

    pos = (t_idx * T + lax.broadcasted_iota(jnp.int32, (ROT_HALF, T), 1)).astype(_F32)
    fidx = lax.broadcasted_iota(jnp.int32, (ROT_HALF, T), 0).astype(_F32)
    inv_freq = jnp.exp(fidx * (-2.0 * math.log(ROPE_THETA) / ROT_DIM))
    ang = pos * inv_freq
    cos = jnp.cos(ang)
    sin = jnp.sin(ang)

    def norm_rope(head, gain):
        xh = qkv_s[head * HEAD_DIM:(head + 1) * HEAD_DIM, :]
        ms = jnp.sum(xh * xh, axis=0, keepdims=True) * (1.0 / HEAD_DIM)
        yh = xh * lax.rsqrt(ms + EPS) * gain
        y1 = yh[0:ROT_HALF]
        y2 = yh[ROT_HALF:ROT_DIM]
        return jnp.concatenate([y1 * cos - y2 * sin, y2 * cos + y1 * sin, yh[ROT_DIM:]], axis=0)

    for h in range(N_Q_HEADS):
        off = h * QPAD + ((h // GROUP) % 2) * HEAD_DIM
        qt_s[off:off + HEAD_DIM, :] = norm_rope(h, qkg_ref[0]).astype(_BF16)
    kt = jnp.concatenate([norm_rope(N_Q_HEADS + j, qkg_ref[1]) for j in range(N_KV_HEADS)], axis=0)
    knat_s[BLOCK:BLOCK + T, :] = kt.T.astype(_BF16)
    vt_s[:, BLOCK:BLOCK + T] = qkv_s[Q_W + KV_W:QKV_W, :].astype(_BF16)

    for blk in range(n_blk):
        for h in range(N_Q_HEADS):
            pair = (h // GROUP) // 2
            kb = knat_s[blk * BLOCK:(blk + 2) * BLOCK, pair * 2 * HEAD_DIM:(pair + 1) * 2 * HEAD_DIM]
            s_s[blk * N_Q_HEADS + h] = jnp.dot(
                kb, qt_s[h * QPAD:(h + 1) * QPAD, blk * BLOCK:(blk + 1) * BLOCK],
                preferred_element_type=_F32)

    for gi in range(T // CONV_GROUP):
        t0 = gi * CONV_GROUP
        base = t0 + (CONV_HALO - (CONV_WIDTH - 1))
        for c0 in range(0, N_COL, CONV_COLS):
            cols = range(c0, c0 + CONV_COLS)
            acc = {}
            xs = {}
            for k in range(CONV_WIDTH):
                for col in cols:
                    wk = cwb_ref[k, :, col * LANES:(col + 1) * LANES]
                    for r in range(CONV_STRIDE):
                        j = k + r
                        if (j, col) not in xs:
                            xs[j, col] = gbuf_s[col, pl.ds(base + j, SUBLANES, stride=CONV_STRIDE), :]
                        term = wk * xs[j, col]
                        acc[r, col] = term if k == 0 else acc[r, col] + term
            for r in range(CONV_STRIDE):
                for col in cols:
                    cv_s[col, pl.ds(t0 + r, SUBLANES, stride=CONV_STRIDE), :] = (
                        acc[r, col] + cb_ref[:, col * LANES:(col + 1) * LANES])

    kj = lax.broadcasted_iota(jnp.int32, (BLOCK, BLOCK), 0)
    qi = lax.broadcasted_iota(jnp.int32, (BLOCK, BLOCK), 1)
    from_prev = kj > qi
    from_prev_bf = kj.astype(_F32).astype(_BF16) > qi.astype(_F32).astype(_BF16)
    zero_bf = jnp.zeros((BLOCK, BLOCK), _BF16)
    no_prev = jnp.where(first, -jnp.inf, 0.0).astype(_F32)
    for u in range(n_blk * N_Q_HEADS):
        s_prev = s_s[u, 0:BLOCK, :]
        if u < N_Q_HEADS:
            s_prev = s_prev + no_prev
        c = jnp.where(from_prev, s_prev, s_s[u, BLOCK:2 * BLOCK, :])
        sink = sinks_ref[u % N_Q_HEADS] * LOG2_E
        m = jnp.maximum(jnp.max(c, axis=0, keepdims=True), sink)
        p = jnp.exp2(c - m)
        l = jnp.sum(p, axis=0, keepdims=True) + jnp.exp2(sink - m)
        p_bf = p.astype(_BF16)
        p_s[u, 0:BLOCK, :] = jnp.where(from_prev_bf, p_bf, zero_bf)
        p_s[u, BLOCK:2 * BLOCK, :] = jnp.where(from_prev_bf, zero_bf, p_bf)
        linv_s[u] = jnp.broadcast_to(1.0 / l, (SUBLANES, BLOCK))

    for blk in range(n_blk):
        for h in range(N_Q_HEADS):
            kv = h // GROUP
            u = blk * N_Q_HEADS + h
            vb = vt_s[kv * HEAD_DIM:(kv + 1) * HEAD_DIM, blk * BLOCK:(blk + 2) * BLOCK]
            o = jnp.dot(vb, p_s[u], preferred_element_type=_F32)
            ot_s[h * HEAD_DIM:(h + 1) * HEAD_DIM, blk * BLOCK:(blk + 1) * BLOCK] = (
                o * linv_s[u, 0:1, :]).astype(_BF16)

    for c in range(T // DOT_ROWS):
        r = _rows(c, DOT_ROWS)
        ya_s[r, :] = lax.dot_general(ot_s[:, r], wao_ref[...], (((0,), (0,)), ((), ())),
                                     preferred_element_type=_F32)

    for i in range(T // ROW_CHUNK):
        r = _rows(i, ROW_CHUNK)
        row = jnp.concatenate([cv_s[col, r, :] for col in range(N_COL)], axis=1)
        mu = jnp.mean(row, axis=-1, keepdims=True)
        d = row - mu
        var = jnp.mean(d * d, axis=-1, keepdims=True)
        yn = d * lax.rsqrt(var + EPS) * lng_ref[...] + lnb_ref[...]
        y_s[r, :] = (yn * jax.nn.sigmoid(yn)).astype(_BF16)

    for c in range(T // DOT_ROWS):
        r = _rows(c, DOT_ROWS)
        yc_s[r, :] = jnp.dot(y_s[r, :], wco_ref[...], preferred_element_type=_F32)

    for i in range(T // ROW_CHUNK):
        r = _rows(i, ROW_CHUNK)
        g0 = jax.nn.sigmoid(gl_s[r, 0:D_MODEL] + gb_ref[:, 0:D_MODEL])
        g1 = jax.nn.sigmoid(gl_s[r, D_MODEL:2 * D_MODEL] + gb_ref[:, D_MODEL:2 * D_MODEL])
        m_s[r, :] = (g0 * ya_s[r, :] + g1 * yc_s[r, :]).astype(_BF16)

    for c in range(T // DOT_ROWS):
        r = _rows(c, DOT_ROWS)
        o_ref[r, :] = x_ref[r, :] + jnp.dot(m_s[r, :], wout_ref[...], preferred_element_type=_F32)

    knat_s[0:BLOCK, :] = knat_s[T:T + BLOCK, :]
    vt_s[:, 0:BLOCK] = vt_s[:, T:T + BLOCK]
    gbuf_s[:, 0:CONV_HALO, :] = gbuf_s[:, T:T + CONV_HALO, :]


def _mlp_kernel(x_ref, n2g_ref, wup_ref, wdn_ref, o_ref, hn_s):
    T = MLP_TILE
    for i in range(T // ROW_CHUNK):
        r = _rows(i, ROW_CHUNK)
        xv = x_ref[r, :]
        ms = jnp.mean(xv * xv, axis=-1, keepdims=True)
        hn_s[r, :] = (xv * lax.rsqrt(ms + EPS) * n2g_ref[...]).astype(_BF16)

    acc = x_ref[...]
    for c in range(D_FF // FF_CHUNK):
        u = jnp.dot(hn_s[...], wup_ref[:, c * FF_CHUNK:(c + 1) * FF_CHUNK], preferred_element_type=_F32)
        u = jnp.maximum(u, 0.0)
        acc = acc + jnp.dot((u * u).astype(_BF16), wdn_ref[c * FF_CHUNK:(c + 1) * FF_CHUNK, :],
                            preferred_element_type=_F32)
    o_ref[...] = acc


def _const_spec(shape):
    zeros = (0,) * len(shape)
    return pl.BlockSpec(shape, lambda *_: zeros, pipeline_mode=pl.Buffered(1))


def _mixer_call(x, sinks, n1g, wqkvt, qkg, wc, wg, gb, wao, cwb, cb, lng, lnb, wco, wout):
    B, S, D = x.shape
    T = SEQ_TILE
    const_args = (n1g, wqkvt, qkg, wc, wg, gb, wao, cwb, cb, lng, lnb, wco, wout)
    n_units = (T // BLOCK) * N_Q_HEADS
    return pl.pallas_call(
        _mixer_kernel,
        out_shape=jax.ShapeDtypeStruct((B, S, D), _F32),
        grid=(B, S // T),
        in_specs=[pl.BlockSpec(memory_space=pltpu.SMEM),
                  pl.BlockSpec((None, T, D), lambda b, t: (b, t, 0))]
                 + [_const_spec(a.shape) for a in const_args],
        out_specs=pl.BlockSpec((None, T, D), lambda b, t: (b, t, 0)),
        scratch_shapes=[
            pltpu.VMEM((T, D), _BF16),
            pltpu.VMEM((QKV_W, T), _F32),
            pltpu.VMEM((N_Q_HEADS * QPAD, T), _BF16),
            pltpu.VMEM((BLOCK + T, KV_W), _BF16),
            pltpu.VMEM((KV_W, BLOCK + T), _BF16),
            pltpu.VMEM((Q_W, T), _BF16),
            pltpu.VMEM((T, 2 * D), _F32),
            pltpu.VMEM((N_COL, CONV_HALO + T, LANES), _F32),
            pltpu.VMEM((N_COL, T, LANES), _F32),
            pltpu.VMEM((T, D), _BF16),
            pltpu.VMEM((T, D), _F32),
            pltpu.VMEM((T, D), _F32),
            pltpu.VMEM((T, D), _BF16),
            pltpu.VMEM((n_units, 2 * BLOCK, BLOCK), _F32),
            pltpu.VMEM((n_units, 2 * BLOCK, BLOCK), _BF16),
            pltpu.VMEM((n_units, SUBLANES, BLOCK), _F32),
            pltpu.VMEM((T, 2 * D), _F32),
        ],
        compiler_params=pltpu.CompilerParams(
            dimension_semantics=("arbitrary", "arbitrary"),
            vmem_limit_bytes=VMEM_LIMIT_BYTES),
        name="mixer",
    )(sinks, x, *const_args)


def _mlp_call(x2d, n2g, wup, wdn):
    N, D = x2d.shape
    T = MLP_TILE
    return pl.pallas_call(
        _mlp_kernel,
        out_shape=jax.ShapeDtypeStruct((N, D), _F32),
        grid=(N // T,),
        in_specs=[pl.BlockSpec((T, D), lambda i: (i, 0)),
                  _const_spec(n2g.shape), _const_spec(wup.shape), _const_spec(wdn.shape)],
        out_specs=pl.BlockSpec((T, D), lambda i: (i, 0)),
        scratch_shapes=[pltpu.VMEM((T, D), _BF16)],
        compiler_params=pltpu.CompilerParams(
            dimension_semantics=("arbitrary",),
            vmem_limit_bytes=VMEM_LIMIT_BYTES),
        name="mlp",
    )(x2d, n2g, wup, wdn)


def kernel(x, norm1_g, w_in, gate_b, q_norm_g, k_norm_g, sinks, w_attn_out, conv_w, conv_b,
           conv_norm_g, conv_norm_b, w_conv_out, w_out, norm2_g, w_up, w_down):
    B, S, D = x.shape
    depth = w_in.shape[0]
    assert D == D_MODEL and S % SEQ_TILE == 0 and (B * S) % MLP_TILE == 0
    c0, c1 = QKV_W, QKV_W + 2 * D_MODEL
    for l in range(depth):
        wqkvt = w_in[l, :, :c0].T.astype(_BF16)
        wc = w_in[l, :, c0:c1].astype(_BF16)
        wg = w_in[l, :, c1:].astype(_BF16)
        qkg = jnp.stack([
            jnp.broadcast_to((q_norm_g[l] * (HEAD_DIM ** -0.5 * LOG2_E))[:, None], (HEAD_DIM, SEQ_TILE)),
            jnp.broadcast_to(k_norm_g[l][:, None], (HEAD_DIM, SEQ_TILE))])
        x = _mixer_call(
            x, sinks[l], norm1_g[l][None], wqkvt, qkg, wc, wg, gate_b[l][None],
            w_attn_out[l].astype(_BF16),
            jnp.broadcast_to(conv_w[l][:, None, :], (CONV_WIDTH, SUBLANES, D_MODEL)),
            conv_b[l][None], conv_norm_g[l][None],
            conv_norm_b[l][None], w_conv_out[l].astype(_BF16), w_out[l].astype(_BF16))
        x = _mlp_call(x.reshape(B * S, D), norm2_g[l][None], w_up[l].astype(_BF16),
                      w_down[l].astype(_BF16)).reshape(B, S, D)
    return x
```

```python
import math

import jax
import jax.numpy as jnp
from jax import lax
from jax.experimental import pallas as pl
from jax.experimental.pallas import tpu as pltpu

D_MODEL = 1024
N_Q_HEADS = 16
N_KV_HEADS = 4
GROUP = N_Q_HEADS // N_KV_HEADS
HEAD_DIM = 64
ROT_DIM = HEAD_DIM // 4
ROT_HALF = ROT_DIM // 2
ROPE_THETA = 500000.0
LOG2_E = math.log2(math.e)
BLOCK = 128
CONV_WIDTH = 31
CONV_HALO = 32
D_FF = 4 * D_MODEL
EPS = 1e-6
Q_W = N_Q_HEADS * HEAD_DIM
KV_W = N_KV_HEADS * HEAD_DIM
QKV_W = Q_W + 2 * KV_W
QPAD = 2 * HEAD_DIM

LANES = 128
SUBLANES = 8
N_COL = D_MODEL // LANES
CONV_STRIDE = 4
CONV_GROUP = CONV_STRIDE * SUBLANES
CONV_COLS = 2

SEQ_TILE = 512
MLP_TILE = 512
FF_CHUNK = 1024
ROW_CHUNK = 16
DOT_ROWS = 128
VMEM_LIMIT_BYTES = 56 * 1024 * 1024

_F32 = jnp.float32
_BF16 = jnp.bfloat16


def _rows(i, n):
    return pl.ds(i * n, n)


def _mixer_kernel(sinks_ref, x_ref, n1g_ref, wqkvt_ref, qkg_ref, wc_ref, wg_ref, gb_ref, wao_ref,
                  cwb_ref, cb_ref, lng_ref, lnb_ref, wco_ref, wout_ref, o_ref,
                  h_s, qkv_s, qt_s, knat_s, vt_s, ot_s, ag_s, gbuf_s, cv_s, y_s, ya_s, yc_s, m_s,
                  s_s, p_s, linv_s, gl_s):
    T = SEQ_TILE
    n_blk = T // BLOCK
    t_idx = pl.program_id(1)
    first = t_idx == 0

    @pl.when(first)
    def _():
        qt_s[...] = jnp.zeros_like(qt_s)
        knat_s[0:BLOCK, :] = jnp.zeros((BLOCK, KV_W), _BF16)
        vt_s[:, 0:BLOCK] = jnp.zeros((KV_W, BLOCK), _BF16)
        gbuf_s[:, 0:CONV_HALO, :] = jnp.zeros((N_COL, CONV_HALO, LANES), _F32)

    for i in range(T // ROW_CHUNK):
        r = _rows(i, ROW_CHUNK)
        xv = x_ref[r, :]
        ms = jnp.mean(xv * xv, axis=-1, keepdims=True)
        h_s[r, :] = (xv * lax.rsqrt(ms + EPS) * n1g_ref[...]).astype(_BF16)

    for c in range(T // DOT_ROWS):
        r = _rows(c, DOT_ROWS)
        ag_s[r, :] = jnp.dot(h_s[r, :], wc_ref[...], preferred_element_type=_F32)
    qkv_s[...] = lax.dot_general(wqkvt_ref[...], h_s[...], (((1,), (1,)), ((), ())),
                                 preferred_element_type=_F32)
    gl_s[...] = jnp.dot(h_s[...], wg_ref[...], preferred_element_type=_F32)

    for i in range(T // ROW_CHUNK):
        r = _rows(i, ROW_CHUNK)
        glu = ag_s[r, 0:D_MODEL] * jax.nn.sigmoid(ag_s[r, D_MODEL:2 * D_MODEL])
        for col in range(N_COL):
            gbuf_s[col, pl.ds(CONV_HALO + i * ROW_CHUNK, ROW_CHUNK), :] = glu[:, col * LANES:(col + 1) * LANES]

    pos = (t_idx * T + lax.broadcasted_iota(jnp.int32, (ROT_HALF, T), 1)).astype(_F32)
    fidx = lax.broadcasted_iota(jnp.int32, (ROT_HALF, T), 0).astype(_F32)
    inv_freq = jnp.exp(fidx * (-2.0 * math.log(ROPE_THETA) / ROT_DIM))
    ang = pos * inv_freq
    cos = jnp.cos(ang)
    sin = jnp.sin(ang)

    def norm_rope(head, gain):
        xh = qkv_s[head * HEAD_DIM:(head + 1) * HEAD_DIM, :]
        ms = jnp.sum(xh * xh, axis=0, keepdims=True) * (1.0 / HEAD_DIM)
        yh = xh * lax.rsqrt(ms + EPS) * gain
        y1 = yh[0:ROT_HALF]
        y2 = yh[ROT_HALF:ROT_DIM]
        return jnp.concatenate([y1 * cos - y2 * sin, y2 * cos + y1 * sin, yh[ROT_DIM:]], axis=0)

    for h in range(N_Q_HEADS):
        off = h * QPAD + ((h // GROUP) % 2) * HEAD_DIM
        qt_s[off:off + HEAD_DIM, :] = norm_rope(h, qkg_ref[0]).astype(_BF16)
    kt = jnp.concatenate([norm_rope(N_Q_HEADS + j, qkg_ref[1]) for j in range(N_KV_HEADS)], axis=0)
    knat_s[BLOCK:BLOCK + T, :] = kt.T.astype(_BF16)
    vt_s[:, BLOCK:BLOCK + T] = qkv_s[Q_W + KV_W:QKV_W, :].astype(_BF16)

    for blk in range(n_blk):
        for h in range(N_Q_HEADS):
            pair = (h // GROUP) // 2
            kb = knat_s[blk * BLOCK:(blk + 2) * BLOCK, pair * 2 * HEAD_DIM:(pair + 1) * 2 * HEAD_DIM]
            s_s[blk * N_Q_HEADS + h] = jnp.dot(
                kb, qt_s[h * QPAD:(h + 1) * QPAD, blk * BLOCK:(blk + 1) * BLOCK],
                preferred_element_type=_F32)

    for gi in range(T // CONV_GROUP):
        t0 = gi * CONV_GROUP
        base = t0 + (CONV_HALO - (CONV_WIDTH - 1))
        for c0 in range(0, N_COL, CONV_COLS):
            cols = range(c0, c0 + CONV_COLS)
            acc = {}
            xs = {}
            for k in range(CONV_WIDTH):
                for col in cols:
                    wk = cwb_ref[k, :, col * LANES:(col + 1) * LANES]
                    for r in range(CONV_STRIDE):
                        j = k + r
                        if (j, col) not in xs:
                            xs[j, col] = gbuf_s[col, pl.ds(base + j, SUBLANES, stride=CONV_STRIDE), :]
                        term = wk * xs[j, col]
                        acc[r, col] = term if k == 0 else acc[r, col] + term
            for r in range(CONV_STRIDE):
                for col in cols:
                    cv_s[col, pl.ds(t0 + r, SUBLANES, stride=CONV_STRIDE), :] = (
                        acc[r, col] + cb_ref[:, col * LANES:(col + 1) * LANES])

    kj = lax.broadcasted_iota(jnp.int32, (BLOCK, BLOCK), 0)
    qi = lax.broadcasted_iota(jnp.int32, (BLOCK, BLOCK), 1)
    from_prev = kj > qi
    from_prev_bf = kj.astype(_F32).astype(_BF16) > qi.astype(_F32).astype(_BF16)
    zero_bf = jnp.zeros((BLOCK, BLOCK), _BF16)
    no_prev = jnp.where(first, -jnp.inf, 0.0).astype(_F32)
    for u in range(n_blk * N_Q_HEADS):
        s_prev = s_s[u, 0:BLOCK, :]
        if u < N_Q_HEADS:
            s_prev = s_prev + no_prev
        c = jnp.where(from_prev, s_prev, s_s[u, BLOCK:2 * BLOCK, :])
        sink = sinks_ref[u % N_Q_HEADS] * LOG2_E
        m = jnp.maximum(jnp.max(c, axis=0, keepdims=True), sink)
        p = jnp.exp2(c - m)
        l = jnp.sum(p, axis=0, keepdims=True) + jnp.exp2(sink - m)
        p_bf = p.astype(_BF16)
        p_s[u, 0:BLOCK, :] = jnp.where(from_prev_bf, p_bf, zero_bf)
        p_s[u, BLOCK:2 * BLOCK, :] = jnp.where(from_prev_bf, zero_bf, p_bf)
        linv_s[u] = jnp.broadcast_to(1.0 / l, (SUBLANES, BLOCK))

    for blk in range(n_blk):
        for h in range(N_Q_HEADS):
            kv = h // GROUP
            u = blk * N_Q_HEADS + h
            vb = vt_s[kv * HEAD_DIM:(kv + 1) * HEAD_DIM, blk * BLOCK:(blk + 2) * BLOCK]
            o = jnp.dot(vb, p_s[u], preferred_element_type=_F32)
            ot_s[h * HEAD_DIM:(h + 1) * HEAD_DIM, blk * BLOCK:(blk + 1) * BLOCK] = (
                o * linv_s[u, 0:1, :]).astype(_BF16)

    for c in range(T // DOT_ROWS):
        r = _rows(c, DOT_ROWS)
        ya_s[r, :] = lax.dot_general(ot_s[:, r], wao_ref[...], (((0,), (0,)), ((), ())),
                                     preferred_element_type=_F32)

    for i in range(T // ROW_CHUNK):
        r = _rows(i, ROW_CHUNK)
        row = jnp.concatenate([cv_s[col, r, :] for col in range(N_COL)], axis=1)
        mu = jnp.mean(row, axis=-1, keepdims=True)
        d = row - mu
        var = jnp.mean(d * d, axis=-1, keepdims=True)
        yn = d * lax.rsqrt(var + EPS) * lng_ref[...] + lnb_ref[...]
        y_s[r, :] = (yn * jax.nn.sigmoid(yn)).astype(_BF16)

    for c in range(T // DOT_ROWS):
        r = _rows(c, DOT_ROWS)
        yc_s[r, :] = jnp.dot(y_s[r, :], wco_ref[...], preferred_element_type=_F32)

    for i in range(T // ROW_CHUNK):
        r = _rows(i, ROW_CHUNK)
        g0 = jax.nn.sigmoid(gl_s[r, 0:D_MODEL] + gb_ref[:, 0:D_MODEL])
        g1 = jax.nn.sigmoid(gl_s[r, D_MODEL:2 * D_MODEL] + gb_ref[:, D_MODEL:2 * D_MODEL])
        m_s[r, :] = (g0 * ya_s[r, :] + g1 * yc_s[r, :]).astype(_BF16)

    for c in range(T // DOT_ROWS):
        r = _rows(c, DOT_ROWS)
        o_ref[r, :] = x_ref[r, :] + jnp.dot(m_s[r, :], wout_ref[...], preferred_element_type=_F32)

    knat_s[0:BLOCK, :] = knat_s[T:T + BLOCK, :]
    vt_s[:, 0:BLOCK] = vt_s[:, T:T + BLOCK]
    gbuf_s[:, 0:CONV_HALO, :] = gbuf_s[:, T:T + CONV_HALO, :]


def _mlp_kernel(x_ref, n2g_ref, wup_ref, wdn_ref, o_ref, hn_s):
    T = MLP_TILE
    for i in range(T // ROW_CHUNK):
        r = _rows(i, ROW_CHUNK)
        xv = x_ref[r, :]
        ms = jnp.mean(xv * xv, axis=-1, keepdims=True)
        hn_s[r, :] = (xv * lax.rsqrt(ms + EPS) * n2g_ref[...]).astype(_BF16)

    acc = x_ref[...]
    for c in range(D_FF // FF_CHUNK):
        u = jnp.dot(hn_s[...], wup_ref[:, c * FF_CHUNK:(c + 1) * FF_CHUNK], preferred_element_type=_F32)
        u = jnp.maximum(u, 0.0)
        acc = acc + jnp.dot((u * u).astype(_BF16), wdn_ref[c * FF_CHUNK:(c + 1) * FF_CHUNK, :],
                            preferred_element_type=_F32)
    o_ref[...] = acc


def _const_spec(shape):
    zeros = (0,) * len(shape)
    return pl.BlockSpec(shape, lambda *_: zeros, pipeline_mode=pl.Buffered(1))


def _mixer_call(x, sinks, n1g, wqkvt, qkg, wc, wg, gb, wao, cwb, cb, lng, lnb, wco, wout):
    B, S, D = x.shape
    T = SEQ_TILE
    const_args = (n1g, wqkvt, qkg, wc, wg, gb, wao, cwb, cb, lng, lnb, wco, wout)
    n_units = (T // BLOCK) * N_Q_HEADS
    return pl.pallas_call(
        _mixer_kernel,
        out_shape=jax.ShapeDtypeStruct((B, S, D), _F32),
        grid=(B, S // T),
        in_specs=[pl.BlockSpec(memory_space=pltpu.SMEM),
                  pl.BlockSpec((None, T, D), lambda b, t: (b, t, 0))]
                 + [_const_spec(a.shape) for a in const_args],
        out_specs=pl.BlockSpec((None, T, D), lambda b, t: (b, t, 0)),
        scratch_shapes=[
            pltpu.VMEM((T, D), _BF16),
            pltpu.VMEM((QKV_W, T), _F32),
            pltpu.VMEM((N_Q_HEADS * QPAD, T), _BF16),
            pltpu.VMEM((BLOCK + T, KV_W), _BF16),
            pltpu.VMEM((KV_W, BLOCK + T), _BF16),
            pltpu.VMEM((Q_W, T), _BF16),
            pltpu.VMEM((T, 2 * D), _F32),
            pltpu.VMEM((N_COL, CONV_HALO + T, LANES), _F32),
            pltpu.VMEM((N_COL, T, LANES), _F32),
            pltpu.VMEM((T, D), _BF16),
            pltpu.VMEM((T, D), _F32),
            pltpu.VMEM((T, D), _F32),
            pltpu.VMEM((T, D), _BF16),
            pltpu.VMEM((n_units, 2 * BLOCK, BLOCK), _F32),
            pltpu.VMEM((n_units, 2 * BLOCK, BLOCK), _BF16),
            pltpu.VMEM((n_units, SUBLANES, BLOCK), _F32),
            pltpu.VMEM((T, 2 * D), _F32),
        ],
        compiler_params=pltpu.CompilerParams(
            dimension_semantics=("arbitrary", "arbitrary"),
            vmem_limit_bytes=VMEM_LIMIT_BYTES),
        name="mixer",
    )(sinks, x, *const_args)


def _mlp_call(x2d, n2g, wup, wdn):
    N, D = x2d.shape
    T = MLP_TILE
    return pl.pallas_call(
        _mlp_kernel,
        out_shape=jax.ShapeDtypeStruct((N, D), _F32),
        grid=(N // T,),
        in_specs=[pl.BlockSpec((T, D), lambda i: (i, 0)),
                  _const_spec(n2g.shape), _const_spec(wup.shape), _const_spec(wdn.shape)],
        out_specs=pl.BlockSpec((T, D), lambda i: (i, 0)),
        scratch_shapes=[pltpu.VMEM((T, D), _BF16)],
        compiler_params=pltpu.CompilerParams(
            dimension_semantics=("arbitrary",),
            vmem_limit_bytes=VMEM_LIMIT_BYTES),
        name="mlp",
    )(x2d, n2g, wup, wdn)


def kernel(x, norm1_g, w_in, gate_b, q_norm_g, k_norm_g, sinks, w_attn_out, conv_w, conv_b,
           conv_norm_g, conv_norm_b, w_conv_out, w_out, norm2_g, w_up, w_down):
    B, S, D = x.shape
    depth = w_in.shape[0]
    assert D == D_MODEL and S % SEQ_TILE == 0 and (B * S) % MLP_TILE == 0
    c0, c1 = QKV_W, QKV_W + 2 * D_MODEL
    for l in range(depth):
        wqkvt = w_in[l, :, :c0].T.astype(_BF16)
        wc = w_in[l, :, c0:c1].astype(_BF16)
        wg = w_in[l, :, c1:].astype(_BF16)
        qkg = jnp.stack([
            jnp.broadcast_to((q_norm_g[l] * (HEAD_DIM ** -0.5 * LOG2_E))[:, None], (HEAD_DIM, SEQ_TILE)),
            jnp.broadcast_to(k_norm_g[l][:, None], (HEAD_DIM, SEQ_TILE))])
        x = _mixer_call(
            x, sinks[l], norm1_g[l][None], wqkvt, qkg, wc, wg, gate_b[l][None],
            w_attn_out[l].astype(_BF16),
            jnp.broadcast_to(conv_w[l][:, None, :], (CONV_WIDTH, SUBLANES, D_MODEL)),
            conv_b[l][None], conv_norm_g[l][None],
            conv_norm_b[l][None], w_conv_out[l].astype(_BF16), w_out[l].astype(_BF16))
        x = _mlp_call(x.reshape(B * S, D), norm2_g[l][None], w_up[l].astype(_BF16),
                      w_down[l].astype(_BF16)).reshape(B, S, D)
    return x
```

```python
import functools
import math

import jax
import jax.numpy as jnp
from jax import lax
from jax.experimental import pallas as pl
from jax.experimental.pallas import tpu as pltpu

D_MODEL = 1024
N_Q_HEADS = 16
N_KV_HEADS = 4
GROUP = N_Q_HEADS // N_KV_HEADS
HEAD_DIM = 64
ROT_DIM = HEAD_DIM // 4
ROT_HALF = ROT_DIM // 2
ROPE_THETA = 500000.0
LOG2_E = math.log2(math.e)
BLOCK = 128
CONV_WIDTH = 31
CONV_HALO = 32
D_FF = 4 * D_MODEL
EPS = 1e-6
Q_W = N_Q_HEADS * HEAD_DIM
KV_W = N_KV_HEADS * HEAD_DIM
QKV_W = Q_W + 2 * KV_W
QPAD = 2 * HEAD_DIM

LANES = 128
SUBLANES = 8
N_COL = D_MODEL // LANES
CONV_STRIDE = 4
CONV_GROUP = CONV_STRIDE * SUBLANES
CONV_COLS = 2

SEQ_TILE = 256
FF_CHUNK = 1024
ROW_CHUNK = 16
DOT_ROWS = 128
VMEM_LIMIT_BYTES = 56 * 1024 * 1024

_F32 = jnp.float32
_BF16 = jnp.bfloat16


def _rows(i, n):
    return pl.ds(i * n, n)


def _block_kernel(n_tiles, sinks_ref, x_ref, n1g_ref, wqkvt_ref, qkg_ref, wc_ref, wg_ref, gb_ref, wao_ref,
                  cwb_ref, cb_ref, lng_ref, lnb_ref, wco_ref, wout_ref, n2g_ref, wup_ref, wdn_ref, o_ref,
                  h_s, qkv_s, qt_s, knat_s, vt_s, ot_s, ag_s, gbuf_s, cv_s, y_s, ya_s, yc_s, m_s,
                  s_s, p_s, linv_s, gl_s, x1_s, x1m_s, hn_s):
    T = SEQ_TILE
    n_blk = T // BLOCK
    g_idx = pl.program_id(0)
    t_idx = lax.rem(g_idx, n_tiles)
    first = t_idx == 0

    @pl.when(g_idx == 0)
    def _():
        x1_s[...] = jnp.zeros_like(x1_s)

    @pl.when(first)
    def _():
        qt_s[...] = jnp.zeros_like(qt_s)
        knat_s[0:BLOCK, :] = jnp.zeros((BLOCK, KV_W), _BF16)
        vt_s[:, 0:BLOCK] = jnp.zeros((KV_W, BLOCK), _BF16)
        gbuf_s[:, 0:CONV_HALO, :] = jnp.zeros((N_COL, CONV_HALO, LANES), _F32)

    for i in range(T // ROW_CHUNK):
        r = _rows(i, ROW_CHUNK)
        xv = x1_s[r, :]
        x1m_s[r, :] = xv
        ms = jnp.mean(xv * xv, axis=-1, keepdims=True)
        hn_s[r, :] = (xv * lax.rsqrt(ms + EPS) * n2g_ref[...]).astype(_BF16)

    def mlp_chunk(c, acc):
        ff = slice(c * FF_CHUNK, (c + 1) * FF_CHUNK)
        u = jnp.maximum(jnp.dot(hn_s[...], wup_ref[:, ff], preferred_element_type=_F32), 0.0)
        return acc + jnp.dot((u * u).astype(_BF16), wdn_ref[ff, :], preferred_element_type=_F32)

    for i in range(T // ROW_CHUNK):
        r = _rows(i, ROW_CHUNK)
        xv = x_ref[r, :]
        ms = jnp.mean(xv * xv, axis=-1, keepdims=True)
        h_s[r, :] = (xv * lax.rsqrt(ms + EPS) * n1g_ref[...]).astype(_BF16)

    for c in range(T // DOT_ROWS):
        r = _rows(c, DOT_ROWS)
        ag_s[r, :] = jnp.dot(h_s[r, :], wc_ref[...], preferred_element_type=_F32)
    qkv_s[...] = lax.dot_general(wqkvt_ref[...], h_s[...], (((1,), (1,)), ((), ())),
                                 preferred_element_type=_F32)
    gl_s[...] = jnp.dot(h_s[...], wg_ref[...], preferred_element_type=_F32)
    mlp_acc = mlp_chunk(0, x1m_s[...])

    for i in range(T // ROW_CHUNK):
        r = _rows(i, ROW_CHUNK)
        glu = ag_s[r, 0:D_MODEL] * jax.nn.sigmoid(ag_s[r, D_MODEL:2 * D_MODEL])
        for col in range(N_COL):
            gbuf_s[col, pl.ds(CONV_HALO + i * ROW_CHUNK, ROW_CHUNK), :] = glu[:, col * LANES:(col + 1) * LANES]

    pos = (t_idx * T + lax.broadcasted_iota(jnp.int32, (ROT_HALF, T), 1)).astype(_F32)
    fidx = lax.broadcasted_iota(jnp.int32, (ROT_HALF, T), 0).astype(_F32)
    inv_freq = jnp.exp(fidx * (-2.0 * math.log(ROPE_THETA) / ROT_DIM))
    ang = pos * inv_freq
    cos = jnp.cos(ang)
    sin = jnp.sin(ang)

    def norm_rope(head, gain):
        xh = qkv_s[head * HEAD_DIM:(head + 1) * HEAD_DIM, :]
        ms = jnp.sum(xh * xh, axis=0, keepdims=True) * (1.0 / HEAD_DIM)
        yh = xh * lax.rsqrt(ms + EPS) * gain
        y1 = yh[0:ROT_HALF]
        y2 = yh[ROT_HALF:ROT_DIM]
        return jnp.concatenate([y1 * cos - y2 * sin, y2 * cos + y1 * sin, yh[ROT_DIM:]], axis=0)

    for h in range(N_Q_HEADS):
        off = h * QPAD + ((h // GROUP) % 2) * HEAD_DIM
        qt_s[off:off + HEAD_DIM, :] = norm_rope(h, qkg_ref[0]).astype(_BF16)
    kt = jnp.concatenate([norm_rope(N_Q_HEADS + j, qkg_ref[1]) for j in range(N_KV_HEADS)], axis=0)
    knat_s[BLOCK:BLOCK + T, :] = kt.T.astype(_BF16)
    vt_s[:, BLOCK:BLOCK + T] = qkv_s[Q_W + KV_W:QKV_W, :].astype(_BF16)

    for blk in range(n_blk):
        for h in range(N_Q_HEADS):
            pair = (h // GROUP) // 2
            kb = knat_s[blk * BLOCK:(blk + 2) * BLOCK, pair * 2 * HEAD_DIM:(pair + 1) * 2 * HEAD_DIM]
            s_s[blk * N_Q_HEADS + h] = jnp.dot(
                kb, qt_s[h * QPAD:(h + 1) * QPAD, blk * BLOCK:(blk + 1) * BLOCK],
                preferred_element_type=_F32)
    mlp_acc = mlp_chunk(1, mlp_acc)

    for gi in range(T // CONV_GROUP):
        t0 = gi * CONV_GROUP
        base = t0 + (CONV_HALO - (CONV_WIDTH - 1))
        for c0 in range(0, N_COL, CONV_COLS):
            cols = range(c0, c0 + CONV_COLS)
            acc = {}
            xs = {}
            for k in range(CONV_WIDTH):
                for col in cols:
                    wk = cwb_ref[k, :, col * LANES:(col + 1) * LANES]
                    for r in range(CONV_STRIDE):
                        j = k + r
                        if (j, col) not in xs:
                            xs[j, col] = gbuf_s[col, pl.ds(base + j, SUBLANES, stride=CONV_STRIDE), :]
                        term = wk * xs[j, col]
                        acc[r, col] = term if k == 0 else acc[r, col] + term
            for r in range(CONV_STRIDE):
                for col in cols:
                    cv_s[col, pl.ds(t0 + r, SUBLANES, stride=CONV_STRIDE), :] = (
                        acc[r, col] + cb_ref[:, col * LANES:(col + 1) * LANES])

    kj = lax.broadcasted_iota(jnp.int32, (BLOCK, BLOCK), 0)
    qi = lax.broadcasted_iota(jnp.int32, (BLOCK, BLOCK), 1)
    from_prev = kj > qi
    from_prev_bf = kj.astype(_F32).astype(_BF16) > qi.astype(_F32).astype(_BF16)
    zero_bf = jnp.zeros((BLOCK, BLOCK), _BF16)
    no_prev = jnp.where(first, -jnp.inf, 0.0).astype(_F32)
    for u in range(n_blk * N_Q_HEADS):
        s_prev = s_s[u, 0:BLOCK, :]
        if u < N_Q_HEADS:
            s_prev = s_prev + no_prev
        c = jnp.where(from_prev, s_prev, s_s[u, BLOCK:2 * BLOCK, :])
        sink = sinks_ref[u % N_Q_HEADS] * LOG2_E
        m = jnp.maximum(jnp.max(c, axis=0, keepdims=True), sink)
        p = jnp.exp2(c - m)
        l = jnp.sum(p, axis=0, keepdims=True) + jnp.exp2(sink - m)
        p_bf = p.astype(_BF16)
        p_s[u, 0:BLOCK, :] = jnp.where(from_prev_bf, p_bf, zero_bf)
        p_s[u, BLOCK:2 * BLOCK, :] = jnp.where(from_prev_bf, zero_bf, p_bf)
        linv_s[u] = jnp.broadcast_to(1.0 / l, (SUBLANES, BLOCK))

    for blk in range(n_blk):
        for h in range(N_Q_HEADS):
            kv = h // GROUP
            u = blk * N_Q_HEADS + h
            vb = vt_s[kv * HEAD_DIM:(kv + 1) * HEAD_DIM, blk * BLOCK:(blk + 2) * BLOCK]
            o = jnp.dot(vb, p_s[u], preferred_element_type=_F32)
            ot_s[h * HEAD_DIM:(h + 1) * HEAD_DIM, blk * BLOCK:(blk + 1) * BLOCK] = (
                o * linv_s[u, 0:1, :]).astype(_BF16)
    mlp_acc = mlp_chunk(2, mlp_acc)

    for c in range(T // DOT_ROWS):
        r = _rows(c, DOT_ROWS)
        ya_s[r, :] = lax.dot_general(ot_s[:, r], wao_ref[...], (((0,), (0,)), ((), ())),
                                     preferred_element_type=_F32)
    mlp_acc = mlp_chunk(3, mlp_acc)
    o_ref[...] = mlp_acc

    for i in range(T // ROW_CHUNK):
        r = _rows(i, ROW_CHUNK)
        row = jnp.concatenate([cv_s[col, r, :] for col in range(N_COL)], axis=1)
        mu = jnp.mean(row, axis=-1, keepdims=True)
        d = row - mu
        var = jnp.mean(d * d, axis=-1, keepdims=True)
        yn = d * lax.rsqrt(var + EPS) * lng_ref[...] + lnb_ref[...]
        y_s[r, :] = (yn * jax.nn.sigmoid(yn)).astype(_BF16)

    for c in range(T // DOT_ROWS):
        r = _rows(c, DOT_ROWS)
        yc_s[r, :] = jnp.dot(y_s[r, :], wco_ref[...], preferred_element_type=_F32)

    for i in range(T // ROW_CHUNK):
        r = _rows(i, ROW_CHUNK)
        g0 = jax.nn.sigmoid(gl_s[r, 0:D_MODEL] + gb_ref[:, 0:D_MODEL])
        g1 = jax.nn.sigmoid(gl_s[r, D_MODEL:2 * D_MODEL] + gb_ref[:, D_MODEL:2 * D_MODEL])
        m_s[r, :] = (g0 * ya_s[r, :] + g1 * yc_s[r, :]).astype(_BF16)

    for c in range(T // DOT_ROWS):
        r = _rows(c, DOT_ROWS)
        x1_s[r, :] = x_ref[r, :] + jnp.dot(m_s[r, :], wout_ref[...], preferred_element_type=_F32)

    knat_s[0:BLOCK, :] = knat_s[T:T + BLOCK, :]
    vt_s[:, 0:BLOCK] = vt_s[:, T:T + BLOCK]
    gbuf_s[:, 0:CONV_HALO, :] = gbuf_s[:, T:T + CONV_HALO, :]


def _const_spec(shape):
    zeros = (0,) * len(shape)
    return pl.BlockSpec(shape, lambda *_: zeros, pipeline_mode=pl.Buffered(1))


def _block_call(x, sinks, *const_args):
    B, S, D = x.shape
    T = SEQ_TILE
    n_tiles = S // T
    n_steps = B * n_tiles
    n_units = (T // BLOCK) * N_Q_HEADS
    out = pl.pallas_call(
        functools.partial(_block_kernel, n_tiles),
        out_shape=jax.ShapeDtypeStruct((B * S, D), _F32),
        grid=(n_steps + 1,),
        in_specs=[pl.BlockSpec(memory_space=pltpu.SMEM),
                  pl.BlockSpec((T, D), lambda g: (jnp.minimum(g, n_steps - 1), 0))]
                 + [_const_spec(a.shape) for a in const_args],
        out_specs=pl.BlockSpec((T, D), lambda g: (jnp.maximum(g - 1, 0), 0)),
        scratch_shapes=[
            pltpu.VMEM((T, D), _BF16),
            pltpu.VMEM((QKV_W, T), _F32),
            pltpu.VMEM((N_Q_HEADS * QPAD, T), _BF16),
            pltpu.VMEM((BLOCK + T, KV_W), _BF16),
            pltpu.VMEM((KV_W, BLOCK + T), _BF16),
            pltpu.VMEM((Q_W, T), _BF16),
            pltpu.VMEM((T, 2 * D), _F32),
            pltpu.VMEM((N_COL, CONV_HALO + T, LANES), _F32),
            pltpu.VMEM((N_COL, T, LANES), _F32),
            pltpu.VMEM((T, D), _BF16),
            pltpu.VMEM((T, D), _F32),
            pltpu.VMEM((T, D), _F32),
            pltpu.VMEM((T, D), _BF16),
            pltpu.VMEM((n_units, 2 * BLOCK, BLOCK), _F32),
            pltpu.VMEM((n_units, 2 * BLOCK, BLOCK), _BF16),
            pltpu.VMEM((n_units, SUBLANES, BLOCK), _F32),
            pltpu.VMEM((T, 2 * D), _F32),
            pltpu.VMEM((T, D), _F32),
            pltpu.VMEM((T, D), _F32),
            pltpu.VMEM((T, D), _BF16),
        ],
        compiler_params=pltpu.CompilerParams(
            dimension_semantics=("arbitrary",),
            vmem_limit_bytes=VMEM_LIMIT_BYTES),
        name="block",
    )(sinks, x.reshape(B * S, D), *const_args)
    return out.reshape(B, S, D)


def kernel(x, norm1_g, w_in, gate_b, q_norm_g, k_norm_g, sinks, w_attn_out, conv_w, conv_b,
           conv_norm_g, conv_norm_b, w_conv_out, w_out, norm2_g, w_up, w_down):
    B, S, D = x.shape
    depth = w_in.shape[0]
    assert D == D_MODEL and S % SEQ_TILE == 0
    c0, c1 = QKV_W, QKV_W + 2 * D_MODEL
    for l in range(depth):
        wqkvt = w_in[l, :, :c0].T.astype(_BF16)
        wc = w_in[l, :, c0:c1].astype(_BF16)
        wg = w_in[l, :, c1:].astype(_BF16)
        qkg = jnp.stack([
            jnp.broadcast_to((q_norm_g[l] * (HEAD_DIM ** -0.5 * LOG2_E))[:, None], (HEAD_DIM, SEQ_TILE)),
            jnp.broadcast_to(k_norm_g[l][:, None], (HEAD_DIM, SEQ_TILE))])
        x = _block_call(
            x, sinks[l], norm1_g[l][None], wqkvt, qkg, wc, wg, gate_b[l][None],
            w_attn_out[l].astype(_BF16),
            jnp.broadcast_to(conv_w[l][:, None, :], (CONV_WIDTH, SUBLANES, D_MODEL)),
            conv_b[l][None], conv_norm_g[l][None],
            conv_norm_b[l][None], w_conv_out[l].astype(_BF16), w_out[l].astype(_BF16),
            norm2_g[l][None], w_up[l].astype(_BF16), w_down[l].astype(_BF16))
    return x
```

```python
import functools
import math

import jax
import jax.numpy as jnp
from jax import lax
from jax.experimental import pallas as pl
from jax.experimental.pallas import tpu as pltpu

D_MODEL = 1024
N_Q_HEADS = 16
N_KV_HEADS = 4
GROUP = N_Q_HEADS // N_KV_HEADS
HEAD_DIM = 64
ROT_DIM = HEAD_DIM // 4
ROT_HALF = ROT_DIM // 2
ROPE_THETA = 500000.0
LOG2_E = math.log2(math.e)
BLOCK = 128
CONV_WIDTH = 31
CONV_HALO = 32
D_FF = 4 * D_MODEL
EPS = 1e-6
Q_W = N_Q_HEADS * HEAD_DIM
KV_W = N_KV_HEADS * HEAD_DIM
QKV_W = Q_W + 2 * KV_W
QPAD = 2 * HEAD_DIM

LANES = 128
SUBLANES = 8
N_COL = D_MODEL // LANES
CONV_STRIDE = 4
CONV_GROUP = CONV_STRIDE * SUBLANES
CONV_COLS = 2

SEQ_TILE = 256
FF_CHUNK = 1024
ROW_CHUNK = 16
DOT_ROWS = 256
VMEM_LIMIT_BYTES = 56 * 1024 * 1024

_F32 = jnp.float32
_BF16 = jnp.bfloat16


def _rows(i, n):
    return pl.ds(i * n, n)


def _block_kernel(n_tiles, sinks_ref, x_ref, n1g_ref, wqkvt_ref, qkg_ref, wc_ref, wg_ref, gb_ref, wao_ref,
                  cwb_ref, cb_ref, lng_ref, lnb_ref, wco_ref, wout_ref, n2g_ref, wup_ref, wdn_ref, o_ref,
                  h_s, qkv_s, qt_s, knat_s, vt_s, ot_s, ag_s, gbuf_s, cv_s, y_s, ya_s, yc_s, m_s,
                  s_s, p_s, linv_s, gl_s, x1_s, x1m_s, hn_s):
    T = SEQ_TILE
    n_blk = T // BLOCK
    g_idx = pl.program_id(0)
    t_idx = lax.rem(g_idx, n_tiles)
    first = t_idx == 0

    @pl.when(g_idx == 0)
    def _():
        x1_s[...] = jnp.zeros_like(x1_s)

    @pl.when(first)
    def _():
        qt_s[...] = jnp.zeros_like(qt_s)
        knat_s[0:BLOCK, :] = jnp.zeros((BLOCK, KV_W), _BF16)
        vt_s[:, 0:BLOCK] = jnp.zeros((KV_W, BLOCK), _BF16)
        gbuf_s[:, 0:CONV_HALO, :] = jnp.zeros((N_COL, CONV_HALO, LANES), _F32)

    for i in range(T // ROW_CHUNK):
        r = _rows(i, ROW_CHUNK)
        xv = x1_s[r, :]
        x1m_s[r, :] = xv
        ms = jnp.mean(xv * xv, axis=-1, keepdims=True)
        hn_s[r, :] = (xv * lax.rsqrt(ms + EPS) * n2g_ref[...]).astype(_BF16)

    def mlp_chunk(c, acc):
        ff = slice(c * FF_CHUNK, (c + 1) * FF_CHUNK)
        u = jnp.maximum(jnp.dot(hn_s[...], wup_ref[:, ff], preferred_element_type=_F32), 0.0)
        return acc + jnp.dot((u * u).astype(_BF16), wdn_ref[ff, :], preferred_element_type=_F32)

    for i in range(T // ROW_CHUNK):
        r = _rows(i, ROW_CHUNK)
        xv = x_ref[r, :]
        ms = jnp.mean(xv * xv, axis=-1, keepdims=True)
        h_s[r, :] = (xv * lax.rsqrt(ms + EPS) * n1g_ref[...]).astype(_BF16)

    for c in range(T // DOT_ROWS):
        r = _rows(c, DOT_ROWS)
        ag_s[r, :] = jnp.dot(h_s[r, :], wc_ref[...], preferred_element_type=_F32)
    qkv_s[...] = lax.dot_general(wqkvt_ref[...], h_s[...], (((1,), (1,)), ((), ())),
                                 preferred_element_type=_F32)
    gl_s[...] = jnp.dot(h_s[...], wg_ref[...], preferred_element_type=_F32)
    mlp_acc = mlp_chunk(0, x1m_s[...])

    for i in range(T // ROW_CHUNK):
        r = _rows(i, ROW_CHUNK)
        glu = ag_s[r, 0:D_MODEL] * jax.nn.sigmoid(ag_s[r, D_MODEL:2 * D_MODEL])
        for col in range(N_COL):
            gbuf_s[col, pl.ds(CONV_HALO + i * ROW_CHUNK, ROW_CHUNK), :] = glu[:, col * LANES:(col + 1) * LANES]

    pos = (t_idx * T + lax.broadcasted_iota(jnp.int32, (ROT_HALF, T), 1)).astype(_F32)
    fidx = lax.broadcasted_iota(jnp.int32, (ROT_HALF, T), 0).astype(_F32)
    inv_freq = jnp.exp(fidx * (-2.0 * math.log(ROPE_THETA) / ROT_DIM))
    ang = pos * inv_freq
    cos = jnp.cos(ang)
    sin = jnp.sin(ang)

    def norm_rope(head, gain):
        xh = qkv_s[head * HEAD_DIM:(head + 1) * HEAD_DIM, :]
        ms = jnp.sum(xh * xh, axis=0, keepdims=True) * (1.0 / HEAD_DIM)
        yh = xh * lax.rsqrt(ms + EPS) * gain
        y1 = yh[0:ROT_HALF]
        y2 = yh[ROT_HALF:ROT_DIM]
        return jnp.concatenate([y1 * cos - y2 * sin, y2 * cos + y1 * sin, yh[ROT_DIM:]], axis=0)

    for h in range(N_Q_HEADS):
        off = h * QPAD + ((h // GROUP) % 2) * HEAD_DIM
        qt_s[off:off + HEAD_DIM, :] = norm_rope(h, qkg_ref[0]).astype(_BF16)
    kt = jnp.concatenate([norm_rope(N_Q_HEADS + j, qkg_ref[1]) for j in range(N_KV_HEADS)], axis=0)
    knat_s[BLOCK:BLOCK + T, :] = kt.T.astype(_BF16)
    vt_s[:, BLOCK:BLOCK + T] = qkv_s[Q_W + KV_W:QKV_W, :].astype(_BF16)

    for blk in range(n_blk):
        for h in range(N_Q_HEADS):
            pair = (h // GROUP) // 2
            kb = knat_s[blk * BLOCK:(blk + 2) * BLOCK, pair * 2 * HEAD_DIM:(pair + 1) * 2 * HEAD_DIM]
            s_s[blk * N_Q_HEADS + h] = jnp.dot(
                kb, qt_s[h * QPAD:(h + 1) * QPAD, blk * BLOCK:(blk + 1) * BLOCK],
                preferred_element_type=_F32)
    mlp_acc = mlp_chunk(1, mlp_acc)

    for gi in range(T // CONV_GROUP):
        t0 = gi * CONV_GROUP
        base = t0 + (CONV_HALO - (CONV_WIDTH - 1))
        for c0 in range(0, N_COL, CONV_COLS):
            cols = range(c0, c0 + CONV_COLS)
            acc = {}
            xs = {}
            for k in range(CONV_WIDTH):
                for col in cols:
                    wk = cwb_ref[k, :, col * LANES:(col + 1) * LANES]
                    for r in range(CONV_STRIDE):
                        j = k + r
                        if (j, col) not in xs:
                            xs[j, col] = gbuf_s[col, pl.ds(base + j, SUBLANES, stride=CONV_STRIDE), :]
                        term = wk * xs[j, col]
                        acc[r, col] = term if k == 0 else acc[r, col] + term
            for r in range(CONV_STRIDE):
                for col in cols:
                    cv_s[col, pl.ds(t0 + r, SUBLANES, stride=CONV_STRIDE), :] = (
                        acc[r, col] + cb_ref[:, col * LANES:(col + 1) * LANES])

    kj = lax.broadcasted_iota(jnp.int32, (BLOCK, BLOCK), 0)
    qi = lax.broadcasted_iota(jnp.int32, (BLOCK, BLOCK), 1)
    from_prev = kj > qi
    from_prev_bf = kj.astype(_F32).astype(_BF16) > qi.astype(_F32).astype(_BF16)
    zero_bf = jnp.zeros((BLOCK, BLOCK), _BF16)
    no_prev = jnp.where(first, -jnp.inf, 0.0).astype(_F32)
    for u in range(n_blk * N_Q_HEADS):
        s_prev = s_s[u, 0:BLOCK, :]
        if u < N_Q_HEADS:
            s_prev = s_prev + no_prev
        c = jnp.where(from_prev, s_prev, s_s[u, BLOCK:2 * BLOCK, :])
        sink = sinks_ref[u % N_Q_HEADS] * LOG2_E
        m = jnp.maximum(jnp.max(c, axis=0, keepdims=True), sink)
        p = jnp.exp2(c - m)
        l = jnp.sum(p, axis=0, keepdims=True) + jnp.exp2(sink - m)
        p_bf = p.astype(_BF16)
        p_s[u, 0:BLOCK, :] = jnp.where(from_prev_bf, p_bf, zero_bf)
        p_s[u, BLOCK:2 * BLOCK, :] = jnp.where(from_prev_bf, zero_bf, p_bf)
        linv_s[u] = jnp.broadcast_to(1.0 / l, (SUBLANES, BLOCK))

    for blk in range(n_blk):
        for h in range(N_Q_HEADS):
            kv = h // GROUP
            u = blk * N_Q_HEADS + h
            vb = vt_s[kv * HEAD_DIM:(kv + 1) * HEAD_DIM, blk * BLOCK:(blk + 2) * BLOCK]
            o = jnp.dot(vb, p_s[u], preferred_element_type=_F32)
            ot_s[h * HEAD_DIM:(h + 1) * HEAD_DIM, blk * BLOCK:(blk + 1) * BLOCK] = (
                o * linv_s[u, 0:1, :]).astype(_BF16)
    mlp_acc = mlp_chunk(2, mlp_acc)

    for c in range(T // DOT_ROWS):
        r = _rows(c, DOT_ROWS)
        ya_s[r, :] = lax.dot_general(ot_s[:, r], wao_ref[...], (((0,), (0,)), ((), ())),
                                     preferred_element_type=_F32)
    mlp_acc = mlp_chunk(3, mlp_acc)
    o_ref[...] = mlp_acc

    for i in range(T // ROW_CHUNK):
        r = _rows(i, ROW_CHUNK)
        row = jnp.concatenate([cv_s[col, r, :] for col in range(N_COL)], axis=1)
        mu = jnp.mean(row, axis=-1, keepdims=True)
        d = row - mu
        var = jnp.mean(d * d, axis=-1, keepdims=True)
        yn = d * lax.rsqrt(var + EPS) * lng_ref[...] + lnb_ref[...]
        y_s[r, :] = (yn * jax.nn.sigmoid(yn)).astype(_BF16)

    for c in range(T // DOT_ROWS):
        r = _rows(c, DOT_ROWS)
        yc_s[r, :] = jnp.dot(y_s[r, :], wco_ref[...], preferred_element_type=_F32)

    for i in range(T // ROW_CHUNK):
        r = _rows(i, ROW_CHUNK)
        g0 = jax.nn.sigmoid(gl_s[r, 0:D_MODEL] + gb_ref[:, 0:D_MODEL])
        g1 = jax.nn.sigmoid(gl_s[r, D_MODEL:2 * D_MODEL] + gb_ref[:, D_MODEL:2 * D_MODEL])
        m_s[r, :] = (g0 * ya_s[r, :] + g1 * yc_s[r, :]).astype(_BF16)

    for c in range(T // DOT_ROWS):
        r = _rows(c, DOT_ROWS)
        x1_s[r, :] = x_ref[r, :] + jnp.dot(m_s[r, :], wout_ref[...], preferred_element_type=_F32)

    knat_s[0:BLOCK, :] = knat_s[T:T + BLOCK, :]
    vt_s[:, 0:BLOCK] = vt_s[:, T:T + BLOCK]
    gbuf_s[:, 0:CONV_HALO, :] = gbuf_s[:, T:T + CONV_HALO, :]


def _const_spec(shape):
    zeros = (0,) * len(shape)
    return pl.BlockSpec(shape, lambda *_: zeros, pipeline_mode=pl.Buffered(1))


def _block_call(x, sinks, *const_args):
    B, S, D = x.shape
    T = SEQ_TILE
    n_tiles = S // T
    n_steps = B * n_tiles
    n_units = (T // BLOCK) * N_Q_HEADS
    out = pl.pallas_call(
        functools.partial(_block_kernel, n_tiles),
        out_shape=jax.ShapeDtypeStruct((B * S, D), _F32),
        grid=(n_steps + 1,),
        in_specs=[pl.BlockSpec(memory_space=pltpu.SMEM),
                  pl.BlockSpec((T, D), lambda g: (jnp.minimum(g, n_steps - 1), 0))]
                 + [_const_spec(a.shape) for a in const_args],
        out_specs=pl.BlockSpec((T, D), lambda g: (jnp.maximum(g - 1, 0), 0)),
        scratch_shapes=[
            pltpu.VMEM((T, D), _BF16),
            pltpu.VMEM((QKV_W, T), _F32),
            pltpu.VMEM((N_Q_HEADS * QPAD, T), _BF16),
            pltpu.VMEM((BLOCK + T, KV_W), _BF16),
            pltpu.VMEM((KV_W, BLOCK + T), _BF16),
            pltpu.VMEM((Q_W, T), _BF16),
            pltpu.VMEM((T, 2 * D), _F32),
            pltpu.VMEM((N_COL, CONV_HALO + T, LANES), _F32),
            pltpu.VMEM((N_COL, T, LANES), _F32),
            pltpu.VMEM((T, D), _BF16),
            pltpu.VMEM((T, D), _F32),
            pltpu.VMEM((T, D), _F32),
            pltpu.VMEM((T, D), _BF16),
            pltpu.VMEM((n_units, 2 * BLOCK, BLOCK), _F32),
            pltpu.VMEM((n_units, 2 * BLOCK, BLOCK), _BF16),
            pltpu.VMEM((n_units, SUBLANES, BLOCK), _F32),
            pltpu.VMEM((T, 2 * D), _F32),
            pltpu.VMEM((T, D), _F32),
            pltpu.VMEM((T, D), _F32),
            pltpu.VMEM((T, D), _BF16),
        ],
        compiler_params=pltpu.CompilerParams(
            dimension_semantics=("arbitrary",),
            vmem_limit_bytes=VMEM_LIMIT_BYTES),
        name="block",
    )(sinks, x.reshape(B * S, D), *const_args)
    return out.reshape(B, S, D)


def kernel(x, norm1_g, w_in, gate_b, q_norm_g, k_norm_g, sinks, w_attn_out, conv_w, conv_b,
           conv_norm_g, conv_norm_b, w_conv_out, w_out, norm2_g, w_up, w_down):
    B, S, D = x.shape
    depth = w_in.shape[0]
    assert D == D_MODEL and S % SEQ_TILE == 0
    c0, c1 = QKV_W, QKV_W + 2 * D_MODEL
    for l in range(depth):
        wqkvt = w_in[l, :, :c0].T.astype(_BF16)
        wc = w_in[l, :, c0:c1].astype(_BF16)
        wg = w_in[l, :, c1:].astype(_BF16)
        qkg = jnp.stack([
            jnp.broadcast_to((q_norm_g[l] * (HEAD_DIM ** -0.5 * LOG2_E))[:, None], (HEAD_DIM, SEQ_TILE)),
            jnp.broadcast_to(k_norm_g[l][:, None], (HEAD_DIM, SEQ_TILE))])
        x = _block_call(
            x, sinks[l], norm1_g[l][None], wqkvt, qkg, wc, wg, gate_b[l][None],
            w_attn_out[l].astype(_BF16),
            jnp.broadcast_to(conv_w[l][:, None, :], (CONV_WIDTH, SUBLANES, D_MODEL)),
            conv_b[l][None], conv_norm_g[l][None],
            conv_norm_b[l][None], w_conv_out[l].astype(_BF16), w_out[l].astype(_BF16),
            norm2_g[l][None], w_up[l].astype(_BF16), w_down[l].astype(_BF16))
    return x
```

```python
import functools
import math

import jax
import jax.numpy as jnp
from jax import lax
from jax.experimental import pallas as pl
from jax.experimental.pallas import tpu as pltpu

D_MODEL = 1024
N_Q_HEADS = 16
N_KV_HEADS = 4
GROUP = N_Q_HEADS // N_KV_HEADS
HEAD_DIM = 64
ROT_DIM = HEAD_DIM // 4
ROT_HALF = ROT_DIM // 2
ROPE_THETA = 500000.0
LOG2_E = math.log2(math.e)
BLOCK = 128
CONV_WIDTH = 31
CONV_HALO = 32
D_FF = 4 * D_MODEL
EPS = 1e-6
Q_W = N_Q_HEADS * HEAD_DIM
KV_W = N_KV_HEADS * HEAD_DIM
QKV_W = Q_W + 2 * KV_W
QPAD = 2 * HEAD_DIM

LANES = 128
SUBLANES = 8
N_COL = D_MODEL // LANES
CONV_STRIDE = 4
CONV_GROUP = CONV_STRIDE * SUBLANES
CONV_COLS = 2

SEQ_TILE = 256
FF_CHUNK = 2048
ROW_CHUNK = 16
DOT_ROWS = 256
VMEM_LIMIT_BYTES = 56 * 1024 * 1024

_F32 = jnp.float32
_BF16 = jnp.bfloat16


def _rows(i, n):
    return pl.ds(i * n, n)


def _block_kernel(n_tiles, sinks_ref, x_ref, n1g_ref, wqkvt_ref, qkg_ref, wc_ref, wg_ref, gb_ref, wao_ref,
                  cwb_ref, cb_ref, lng_ref, lnb_ref, wco_ref, wout_ref, n2g_ref, wup_ref, wdn_ref, o_ref,
                  h_s, qkv_s, qt_s, knat_s, vt_s, ot_s, ag_s, gbuf_s, cv_s, y_s, ya_s, yc_s, m_s,
                  s_s, p_s, linv_s, gl_s, x1_s, x1m_s, hn_s):
    T = SEQ_TILE
    n_blk = T // BLOCK
    g_idx = pl.program_id(0)
    t_idx = lax.rem(g_idx, n_tiles)
    first = t_idx == 0

    @pl.when(g_idx == 0)
    def _():
        x1_s[...] = jnp.zeros_like(x1_s)

    @pl.when(first)
    def _():
        qt_s[...] = jnp.zeros_like(qt_s)
        knat_s[0:BLOCK, :] = jnp.zeros((BLOCK, KV_W), _BF16)
        vt_s[:, 0:BLOCK] = jnp.zeros((KV_W, BLOCK), _BF16)
        gbuf_s[:, 0:CONV_HALO, :] = jnp.zeros((N_COL, CONV_HALO, LANES), _F32)

    for i in range(T // ROW_CHUNK):
        r = _rows(i, ROW_CHUNK)
        xv = x1_s[r, :]
        x1m_s[r, :] = xv
        ms = jnp.mean(xv * xv, axis=-1, keepdims=True)
        hn_s[r, :] = (xv * lax.rsqrt(ms + EPS) * n2g_ref[...]).astype(_BF16)

    def mlp_chunk(c, acc):
        ff = slice(c * FF_CHUNK, (c + 1) * FF_CHUNK)
        u = jnp.maximum(jnp.dot(hn_s[...], wup_ref[:, ff], preferred_element_type=_F32), 0.0)
        return acc + jnp.dot((u * u).astype(_BF16), wdn_ref[ff, :], preferred_element_type=_F32)

    for i in range(T // ROW_CHUNK):
        r = _rows(i, ROW_CHUNK)
        xv = x_ref[r, :]
        ms = jnp.mean(xv * xv, axis=-1, keepdims=True)
        h_s[r, :] = (xv * lax.rsqrt(ms + EPS) * n1g_ref[...]).astype(_BF16)

    for c in range(T // DOT_ROWS):
        r = _rows(c, DOT_ROWS)
        ag_s[r, :] = jnp.dot(h_s[r, :], wc_ref[...], preferred_element_type=_F32)
    qkv_s[...] = lax.dot_general(wqkvt_ref[...], h_s[...], (((1,), (1,)), ((), ())),
                                 preferred_element_type=_F32)
    gl_s[...] = jnp.dot(h_s[...], wg_ref[...], preferred_element_type=_F32)
    mlp_acc = mlp_chunk(0, x1m_s[...])

    for i in range(T // ROW_CHUNK):
        r = _rows(i, ROW_CHUNK)
        glu = ag_s[r, 0:D_MODEL] * jax.nn.sigmoid(ag_s[r, D_MODEL:2 * D_MODEL])
        for col in range(N_COL):
            gbuf_s[col, pl.ds(CONV_HALO + i * ROW_CHUNK, ROW_CHUNK), :] = glu[:, col * LANES:(col + 1) * LANES]

    pos = (t_idx * T + lax.broadcasted_iota(jnp.int32, (ROT_HALF, T), 1)).astype(_F32)
    fidx = lax.broadcasted_iota(jnp.int32, (ROT_HALF, T), 0).astype(_F32)
    inv_freq = jnp.exp(fidx * (-2.0 * math.log(ROPE_THETA) / ROT_DIM))
    ang = pos * inv_freq
    cos = jnp.cos(ang)
    sin = jnp.sin(ang)

    def norm_rope(head, gain):
        xh = qkv_s[head * HEAD_DIM:(head + 1) * HEAD_DIM, :]
        ms = jnp.sum(xh * xh, axis=0, keepdims=True) * (1.0 / HEAD_DIM)
        yh = xh * lax.rsqrt(ms + EPS) * gain
        y1 = yh[0:ROT_HALF]
        y2 = yh[ROT_HALF:ROT_DIM]
        return jnp.concatenate([y1 * cos - y2 * sin, y2 * cos + y1 * sin, yh[ROT_DIM:]], axis=0)

    for h in range(N_Q_HEADS):
        off = h * QPAD + ((h // GROUP) % 2) * HEAD_DIM
        qt_s[off:off + HEAD_DIM, :] = norm_rope(h, qkg_ref[0]).astype(_BF16)
    kt = jnp.concatenate([norm_rope(N_Q_HEADS + j, qkg_ref[1]) for j in range(N_KV_HEADS)], axis=0)
    knat_s[BLOCK:BLOCK + T, :] = kt.T.astype(_BF16)
    vt_s[:, BLOCK:BLOCK + T] = qkv_s[Q_W + KV_W:QKV_W, :].astype(_BF16)

    for blk in range(n_blk):
        for h in range(N_Q_HEADS):
            pair = (h // GROUP) // 2
            kb = knat_s[blk * BLOCK:(blk + 2) * BLOCK, pair * 2 * HEAD_DIM:(pair + 1) * 2 * HEAD_DIM]
            s_s[blk * N_Q_HEADS + h] = jnp.dot(
                kb, qt_s[h * QPAD:(h + 1) * QPAD, blk * BLOCK:(blk + 1) * BLOCK],
                preferred_element_type=_F32)

    for gi in range(T // CONV_GROUP):
        t0 = gi * CONV_GROUP
        base = t0 + (CONV_HALO - (CONV_WIDTH - 1))
        for c0 in range(0, N_COL, CONV_COLS):
            cols = range(c0, c0 + CONV_COLS)
            acc = {}
            xs = {}
            for k in range(CONV_WIDTH):
                for col in cols:
                    wk = cwb_ref[k, :, col * LANES:(col + 1) * LANES]
                    for r in range(CONV_STRIDE):
                        j = k + r
                        if (j, col) not in xs:
                            xs[j, col] = gbuf_s[col, pl.ds(base + j, SUBLANES, stride=CONV_STRIDE), :]
                        term = wk * xs[j, col]
                        acc[r, col] = term if k == 0 else acc[r, col] + term
            for r in range(CONV_STRIDE):
                for col in cols:
                    cv_s[col, pl.ds(t0 + r, SUBLANES, stride=CONV_STRIDE), :] = (
                        acc[r, col] + cb_ref[:, col * LANES:(col + 1) * LANES])

    kj = lax.broadcasted_iota(jnp.int32, (BLOCK, BLOCK), 0)
    qi = lax.broadcasted_iota(jnp.int32, (BLOCK, BLOCK), 1)
    from_prev = kj > qi
    from_prev_bf = kj.astype(_F32).astype(_BF16) > qi.astype(_F32).astype(_BF16)
    zero_bf = jnp.zeros((BLOCK, BLOCK), _BF16)
    no_prev = jnp.where(first, -jnp.inf, 0.0).astype(_F32)
    for u in range(n_blk * N_Q_HEADS):
        s_prev = s_s[u, 0:BLOCK, :]
        if u < N_Q_HEADS:
            s_prev = s_prev + no_prev
        c = jnp.where(from_prev, s_prev, s_s[u, BLOCK:2 * BLOCK, :])
        sink = sinks_ref[u % N_Q_HEADS] * LOG2_E
        m = jnp.maximum(jnp.max(c, axis=0, keepdims=True), sink)
        p = jnp.exp2(c - m)
        l = jnp.sum(p, axis=0, keepdims=True) + jnp.exp2(sink - m)
        p_bf = p.astype(_BF16)
        p_s[u, 0:BLOCK, :] = jnp.where(from_prev_bf, p_bf, zero_bf)
        p_s[u, BLOCK:2 * BLOCK, :] = jnp.where(from_prev_bf, zero_bf, p_bf)
        linv_s[u] = jnp.broadcast_to(1.0 / l, (SUBLANES, BLOCK))

    for blk in range(n_blk):
        for h in range(N_Q_HEADS):
            kv = h // GROUP
            u = blk * N_Q_HEADS + h
            vb = vt_s[kv * HEAD_DIM:(kv + 1) * HEAD_DIM, blk * BLOCK:(blk + 2) * BLOCK]
            o = jnp.dot(vb, p_s[u], preferred_element_type=_F32)
            ot_s[h * HEAD_DIM:(h + 1) * HEAD_DIM, blk * BLOCK:(blk + 1) * BLOCK] = (
                o * linv_s[u, 0:1, :]).astype(_BF16)
    mlp_acc = mlp_chunk(1, mlp_acc)
    o_ref[...] = mlp_acc

    for c in range(T // DOT_ROWS):
        r = _rows(c, DOT_ROWS)
        ya_s[r, :] = lax.dot_general(ot_s[:, r], wao_ref[...], (((0,), (0,)), ((), ())),
                                     preferred_element_type=_F32)

    for i in range(T // ROW_CHUNK):
        r = _rows(i, ROW_CHUNK)
        row = jnp.concatenate([cv_s[col, r, :] for col in range(N_COL)], axis=1)
        mu = jnp.mean(row, axis=-1, keepdims=True)
        d = row - mu
        var = jnp.mean(d * d, axis=-1, keepdims=True)
        yn = d * lax.rsqrt(var + EPS) * lng_ref[...] + lnb_ref[...]
        y_s[r, :] = (yn * jax.nn.sigmoid(yn)).astype(_BF16)

    for c in range(T // DOT_ROWS):
        r = _rows(c, DOT_ROWS)
        yc_s[r, :] = jnp.dot(y_s[r, :], wco_ref[...], preferred_element_type=_F32)

    for i in range(T // ROW_CHUNK):
        r = _rows(i, ROW_CHUNK)
        g0 = jax.nn.sigmoid(gl_s[r, 0:D_MODEL] + gb_ref[:, 0:D_MODEL])
        g1 = jax.nn.sigmoid(gl_s[r, D_MODEL:2 * D_MODEL] + gb_ref[:, D_MODEL:2 * D_MODEL])
        m_s[r, :] = (g0 * ya_s[r, :] + g1 * yc_s[r, :]).astype(_BF16)

    for c in range(T // DOT_ROWS):
        r = _rows(c, DOT_ROWS)
        x1_s[r, :] = x_ref[r, :] + jnp.dot(m_s[r, :], wout_ref[...], preferred_element_type=_F32)

    knat_s[0:BLOCK, :] = knat_s[T:T + BLOCK, :]
    vt_s[:, 0:BLOCK] = vt_s[:, T:T + BLOCK]
    gbuf_s[:, 0:CONV_HALO, :] = gbuf_s[:, T:T + CONV_HALO, :]


def _const_spec(shape):
    zeros = (0,) * len(shape)
    return pl.BlockSpec(shape, lambda *_: zeros, pipeline_mode=pl.Buffered(1))


def _block_call(x, sinks, *const_args):
    B, S, D = x.shape
    T = SEQ_TILE
    n_tiles = S // T
    n_steps = B * n_tiles
    n_units = (T // BLOCK) * N_Q_HEADS
    out = pl.pallas_call(
        functools.partial(_block_kernel, n_tiles),
        out_shape=jax.ShapeDtypeStruct((B * S, D), _F32),
        grid=(n_steps + 1,),
        in_specs=[pl.BlockSpec(memory_space=pltpu.SMEM),
                  pl.BlockSpec((T, D), lambda g: (jnp.minimum(g, n_steps - 1), 0))]
                 + [_const_spec(a.shape) for a in const_args],
        out_specs=pl.BlockSpec((T, D), lambda g: (jnp.maximum(g - 1, 0), 0)),
        scratch_shapes=[
            pltpu.VMEM((T, D), _BF16),
            pltpu.VMEM((QKV_W, T), _F32),
            pltpu.VMEM((N_Q_HEADS * QPAD, T), _BF16),
            pltpu.VMEM((BLOCK + T, KV_W), _BF16),
            pltpu.VMEM((KV_W, BLOCK + T), _BF16),
            pltpu.VMEM((Q_W, T), _BF16),
            pltpu.VMEM((T, 2 * D), _F32),
            pltpu.VMEM((N_COL, CONV_HALO + T, LANES), _F32),
            pltpu.VMEM((N_COL, T, LANES), _F32),
            pltpu.VMEM((T, D), _BF16),
            pltpu.VMEM((T, D), _F32),
            pltpu.VMEM((T, D), _F32),
            pltpu.VMEM((T, D), _BF16),
            pltpu.VMEM((n_units, 2 * BLOCK, BLOCK), _F32),
            pltpu.VMEM((n_units, 2 * BLOCK, BLOCK), _BF16),
            pltpu.VMEM((n_units, SUBLANES, BLOCK), _F32),
            pltpu.VMEM((T, 2 * D), _F32),
            pltpu.VMEM((T, D), _F32),
            pltpu.VMEM((T, D), _F32),
            pltpu.VMEM((T, D), _BF16),
        ],
        compiler_params=pltpu.CompilerParams(
            dimension_semantics=("arbitrary",),
            vmem_limit_bytes=VMEM_LIMIT_BYTES),
        name="block",
    )(sinks, x.reshape(B * S, D), *const_args)
    return out.reshape(B, S, D)


def kernel(x, norm1_g, w_in, gate_b, q_norm_g, k_norm_g, sinks, w_attn_out, conv_w, conv_b,
           conv_norm_g, conv_norm_b, w_conv_out, w_out, norm2_g, w_up, w_down):
    B, S, D = x.shape
    depth = w_in.shape[0]
    assert D == D_MODEL and S % SEQ_TILE == 0
    c0, c1 = QKV_W, QKV_W + 2 * D_MODEL
    for l in range(depth):
        wqkvt = w_in[l, :, :c0].T.astype(_BF16)
        wc = w_in[l, :, c0:c1].astype(_BF16)
        wg = w_in[l, :, c1:].astype(_BF16)
        qkg = jnp.stack([
            jnp.broadcast_to((q_norm_g[l] * (HEAD_DIM ** -0.5 * LOG2_E))[:, None], (HEAD_DIM, SEQ_TILE)),
            jnp.broadcast_to(k_norm_g[l][:, None], (HEAD_DIM, SEQ_TILE))])
        x = _block_call(
            x, sinks[l], norm1_g[l][None], wqkvt, qkg, wc, wg, gate_b[l][None],
            w_attn_out[l].astype(_BF16),
            jnp.broadcast_to(conv_w[l][:, None, :], (CONV_WIDTH, SUBLANES, D_MODEL)),
            conv_b[l][None], conv_norm_g[l][None],
            conv_norm_b[l][None], w_conv_out[l].astype(_BF16), w_out[l].astype(_BF16),
            norm2_g[l][None], w_up[l].astype(_BF16), w_down[l].astype(_BF16))
    return x
```

```python
import functools
import math

import jax
import jax.numpy as jnp
from jax import lax
from jax.experimental import pallas as pl
from jax.experimental.pallas import tpu as pltpu

D_MODEL = 1024
N_Q_HEADS = 16
N_KV_HEADS = 4
GROUP = N_Q_HEADS // N_KV_HEADS
HEAD_DIM = 64
ROT_DIM = HEAD_DIM // 4
ROT_HALF = ROT_DIM // 2
ROPE_THETA = 500000.0
LOG2_E = math.log2(math.e)
BLOCK = 128
CONV_WIDTH = 31
CONV_HALO = 32
D_FF = 4 * D_MODEL
EPS = 1e-6
Q_W = N_Q_HEADS * HEAD_DIM
KV_W = N_KV_HEADS * HEAD_DIM
QKV_W = Q_W + 2 * KV_W
QPAD = 2 * HEAD_DIM

LANES = 128
SUBLANES = 8
N_COL = D_MODEL // LANES
CONV_STRIDE = 4
CONV_GROUP = CONV_STRIDE * SUBLANES
CONV_COLS = 2

SEQ_TILE = 256
FF_CHUNK = 1024
ROW_CHUNK = 16
DOT_ROWS = 256
VMEM_LIMIT_BYTES = 56 * 1024 * 1024

_F32 = jnp.float32
_BF16 = jnp.bfloat16


def _rows(i, n):
    return pl.ds(i * n, n)


def _block_kernel(n_tiles, sinks_ref, x_ref, n1g_ref, wqkvt_ref, qkg_ref, wc_ref, wg_ref, gb_ref, wao_ref,
                  cwb_ref, cb_ref, lng_ref, lnb_ref, wco_ref, wout_ref, n2g_ref, wup_ref, wdn_ref, o_ref,
                  h_s, qkv_s, qt_s, knat_s, vt_s, ot_s, ag_s, gbuf_s, cv_s, y_s, ya_s, yc_s, m_s,
                  s_s, p_s, linv_s, gl_s, x1_s, hn_s):
    T = SEQ_TILE
    n_blk = T // BLOCK
    g_idx = pl.program_id(0)
    t_idx = lax.rem(g_idx, n_tiles)
    first = t_idx == 0

    @pl.when(g_idx == 0)
    def _():
        x1_s[...] = jnp.zeros_like(x1_s)

    @pl.when(first)
    def _():
        qt_s[...] = jnp.zeros_like(qt_s)
        knat_s[0:BLOCK, :] = jnp.zeros((BLOCK, KV_W), _BF16)
        vt_s[:, 0:BLOCK] = jnp.zeros((KV_W, BLOCK), _BF16)
        gbuf_s[:, 0:CONV_HALO, :] = jnp.zeros((N_COL, CONV_HALO, LANES), _F32)

    for i in range(T // ROW_CHUNK):
        r = _rows(i, ROW_CHUNK)
        xv = x1_s[r, :]
        ms = jnp.mean(xv * xv, axis=-1, keepdims=True)
        hn_s[r, :] = (xv * lax.rsqrt(ms + EPS) * n2g_ref[...]).astype(_BF16)

    def mlp_chunk(c, acc):
        ff = slice(c * FF_CHUNK, (c + 1) * FF_CHUNK)
        u = jnp.maximum(jnp.dot(hn_s[...], wup_ref[:, ff], preferred_element_type=_F32), 0.0)
        return acc + jnp.dot((u * u).astype(_BF16), wdn_ref[ff, :], preferred_element_type=_F32)

    for i in range(T // ROW_CHUNK):
        r = _rows(i, ROW_CHUNK)
        xv = x_ref[r, :]
        ms = jnp.mean(xv * xv, axis=-1, keepdims=True)
        h_s[r, :] = (xv * lax.rsqrt(ms + EPS) * n1g_ref[...]).astype(_BF16)

    for c in range(T // DOT_ROWS):
        r = _rows(c, DOT_ROWS)
        ag_s[r, :] = jnp.dot(h_s[r, :], wc_ref[...], preferred_element_type=_F32)
    qkv_s[...] = lax.dot_general(wqkvt_ref[...], h_s[...], (((1,), (1,)), ((), ())),
                                 preferred_element_type=_F32)
    gl_s[...] = jnp.dot(h_s[...], wg_ref[...], preferred_element_type=_F32)
    mlp_acc = mlp_chunk(0, x1_s[...])

    for i in range(T // ROW_CHUNK):
        r = _rows(i, ROW_CHUNK)
        glu = ag_s[r, 0:D_MODEL] * jax.nn.sigmoid(ag_s[r, D_MODEL:2 * D_MODEL])
        for col in range(N_COL):
            gbuf_s[col, pl.ds(CONV_HALO + i * ROW_CHUNK, ROW_CHUNK), :] = glu[:, col * LANES:(col + 1) * LANES]

    pos = (t_idx * T + lax.broadcasted_iota(jnp.int32, (ROT_HALF, T), 1)).astype(_F32)
    fidx = lax.broadcasted_iota(jnp.int32, (ROT_HALF, T), 0).astype(_F32)
    inv_freq = jnp.exp(fidx * (-2.0 * math.log(ROPE_THETA) / ROT_DIM))
    ang = pos * inv_freq
    cos = jnp.cos(ang)
    sin = jnp.sin(ang)

    def norm_rope(head, gain):
        xh = qkv_s[head * HEAD_DIM:(head + 1) * HEAD_DIM, :]
        ms = jnp.sum(xh * xh, axis=0, keepdims=True) * (1.0 / HEAD_DIM)
        yh = xh * lax.rsqrt(ms + EPS) * gain
        y1 = yh[0:ROT_HALF]
        y2 = yh[ROT_HALF:ROT_DIM]
        return jnp.concatenate([y1 * cos - y2 * sin, y2 * cos + y1 * sin, yh[ROT_DIM:]], axis=0)

    for h in range(N_Q_HEADS):
        off = h * QPAD + ((h // GROUP) % 2) * HEAD_DIM
        qt_s[off:off + HEAD_DIM, :] = norm_rope(h, qkg_ref[0]).astype(_BF16)
    kt = jnp.concatenate([norm_rope(N_Q_HEADS + j, qkg_ref[1]) for j in range(N_KV_HEADS)], axis=0)
    knat_s[BLOCK:BLOCK + T, :] = kt.T.astype(_BF16)
    vt_s[:, BLOCK:BLOCK + T] = qkv_s[Q_W + KV_W:QKV_W, :].astype(_BF16)

    for blk in range(n_blk):
        for h in range(N_Q_HEADS):
            pair = (h // GROUP) // 2
            kb = knat_s[blk * BLOCK:(blk + 2) * BLOCK, pair * 2 * HEAD_DIM:(pair + 1) * 2 * HEAD_DIM]
            s_s[blk * N_Q_HEADS + h] = jnp.dot(
                kb, qt_s[h * QPAD:(h + 1) * QPAD, blk * BLOCK:(blk + 1) * BLOCK],
                preferred_element_type=_F32)
    mlp_acc = mlp_chunk(1, mlp_acc)

    for gi in range(T // CONV_GROUP):
        t0 = gi * CONV_GROUP
        base = t0 + (CONV_HALO - (CONV_WIDTH - 1))
        for c0 in range(0, N_COL, CONV_COLS):
            cols = range(c0, c0 + CONV_COLS)
            acc = {}
            xs = {}
            for k in range(CONV_WIDTH):
                for col in cols:
                    wk = cwb_ref[k, :, col * LANES:(col + 1) * LANES]
                    for r in range(CONV_STRIDE):
                        j = k + r
                        if (j, col) not in xs:
                            xs[j, col] = gbuf_s[col, pl.ds(base + j, SUBLANES, stride=CONV_STRIDE), :]
                        term = wk * xs[j, col]
                        acc[r, col] = term if k == 0 else acc[r, col] + term
            for r in range(CONV_STRIDE):
                for col in cols:
                    cv_s[col, pl.ds(t0 + r, SUBLANES, stride=CONV_STRIDE), :] = (
                        acc[r, col] + cb_ref[:, col * LANES:(col + 1) * LANES])

    kj = lax.broadcasted_iota(jnp.int32, (BLOCK, BLOCK), 0)
    qi = lax.broadcasted_iota(jnp.int32, (BLOCK, BLOCK), 1)
    from_prev = kj > qi
    from_prev_bf = kj.astype(_F32).astype(_BF16) > qi.astype(_F32).astype(_BF16)
    zero_bf = jnp.zeros((BLOCK, BLOCK), _BF16)
    no_prev = jnp.where(first, -jnp.inf, 0.0).astype(_F32)
    for u in range(n_blk * N_Q_HEADS):
        s_prev = s_s[u, 0:BLOCK, :]
        if u < N_Q_HEADS:
            s_prev = s_prev + no_prev
        c = jnp.where(from_prev, s_prev, s_s[u, BLOCK:2 * BLOCK, :])
        sink = sinks_ref[u % N_Q_HEADS] * LOG2_E
        m = jnp.maximum(jnp.max(c, axis=0, keepdims=True), sink)
        p = jnp.exp2(c - m)
        l = jnp.sum(p, axis=0, keepdims=True) + jnp.exp2(sink - m)
        p_bf = p.astype(_BF16)
        p_s[u, 0:BLOCK, :] = jnp.where(from_prev_bf, p_bf, zero_bf)
        p_s[u, BLOCK:2 * BLOCK, :] = jnp.where(from_prev_bf, zero_bf, p_bf)
        linv_s[u] = jnp.broadcast_to(1.0 / l, (SUBLANES, BLOCK))

    for blk in range(n_blk):
        for h in range(N_Q_HEADS):
            kv = h // GROUP
            u = blk * N_Q_HEADS + h
            vb = vt_s[kv * HEAD_DIM:(kv + 1) * HEAD_DIM, blk * BLOCK:(blk + 2) * BLOCK]
            o = jnp.dot(vb, p_s[u], preferred_element_type=_F32)
            ot_s[h * HEAD_DIM:(h + 1) * HEAD_DIM, blk * BLOCK:(blk + 1) * BLOCK] = (
                o * linv_s[u, 0:1, :]).astype(_BF16)
    mlp_acc = mlp_chunk(2, mlp_acc)

    for c in range(T // DOT_ROWS):
        r = _rows(c, DOT_ROWS)
        ya_s[r, :] = lax.dot_general(ot_s[:, r], wao_ref[...], (((0,), (0,)), ((), ())),
                                     preferred_element_type=_F32)
    mlp_acc = mlp_chunk(3, mlp_acc)
    o_ref[...] = mlp_acc

    for i in range(T // ROW_CHUNK):
        r = _rows(i, ROW_CHUNK)
        row = jnp.concatenate([cv_s[col, r, :] for col in range(N_COL)], axis=1)
        mu = jnp.mean(row, axis=-1, keepdims=True)
        d = row - mu
        var = jnp.mean(d * d, axis=-1, keepdims=True)
        yn = d * lax.rsqrt(var + EPS) * lng_ref[...] + lnb_ref[...]
        y_s[r, :] = (yn * jax.nn.sigmoid(yn)).astype(_BF16)

    for c in range(T // DOT_ROWS):
        r = _rows(c, DOT_ROWS)
        yc_s[r, :] = jnp.dot(y_s[r, :], wco_ref[...], preferred_element_type=_F32)

    for i in range(T // ROW_CHUNK):
        r = _rows(i, ROW_CHUNK)
        g0 = jax.nn.sigmoid(gl_s[r, 0:D_MODEL] + gb_ref[:, 0:D_MODEL])
        g1 = jax.nn.sigmoid(gl_s[r, D_MODEL:2 * D_MODEL] + gb_ref[:, D_MODEL:2 * D_MODEL])
        m_s[r, :] = (g0 * ya_s[r, :] + g1 * yc_s[r, :]).astype(_BF16)

    for c in range(T // DOT_ROWS):
        r = _rows(c, DOT_ROWS)
        x1_s[r, :] = x_ref[r, :] + jnp.dot(m_s[r, :], wout_ref[...], preferred_element_type=_F32)

    knat_s[0:BLOCK, :] = knat_s[T:T + BLOCK, :]
    vt_s[:, 0:BLOCK] = vt_s[:, T:T + BLOCK]
    gbuf_s[:, 0:CONV_HALO, :] = gbuf_s[:, T:T + CONV_HALO, :]


def _const_spec(shape):
    zeros = (0,) * len(shape)
    return pl.BlockSpec(shape, lambda *_: zeros, pipeline_mode=pl.Buffered(1))


def _block_call(x, sinks, *const_args):
    B, S, D = x.shape
    T = SEQ_TILE
    n_tiles = S // T
    n_steps = B * n_tiles
    n_units = (T // BLOCK) * N_Q_HEADS
    out = pl.pallas_call(
        functools.partial(_block_kernel, n_tiles),
        out_shape=jax.ShapeDtypeStruct((B * S, D), _F32),
        grid=(n_steps + 1,),
        in_specs=[pl.BlockSpec(memory_space=pltpu.SMEM),
                  pl.BlockSpec((T, D), lambda g: (jnp.minimum(g, n_steps - 1), 0))]
                 + [_const_spec(a.shape) for a in const_args],
        out_specs=pl.BlockSpec((T, D), lambda g: (jnp.maximum(g - 1, 0), 0)),
        scratch_shapes=[
            pltpu.VMEM((T, D), _BF16),
            pltpu.VMEM((QKV_W, T), _F32),
            pltpu.VMEM((N_Q_HEADS * QPAD, T), _BF16),
            pltpu.VMEM((BLOCK + T, KV_W), _BF16),
            pltpu.VMEM((KV_W, BLOCK + T), _BF16),
            pltpu.VMEM((Q_W, T), _BF16),
            pltpu.VMEM((T, 2 * D), _F32),
            pltpu.VMEM((N_COL, CONV_HALO + T, LANES), _F32),
            pltpu.VMEM((N_COL, T, LANES), _F32),
            pltpu.VMEM((T, D), _BF16),
            pltpu.VMEM((T, D), _F32),
            pltpu.VMEM((T, D), _F32),
            pltpu.VMEM((T, D), _BF16),
            pltpu.VMEM((n_units, 2 * BLOCK, BLOCK), _F32),
            pltpu.VMEM((n_units, 2 * BLOCK, BLOCK), _BF16),
            pltpu.VMEM((n_units, SUBLANES, BLOCK), _F32),
            pltpu.VMEM((T, 2 * D), _F32),
            pltpu.VMEM((T, D), _F32),
            pltpu.VMEM((T, D), _BF16),
        ],
        compiler_params=pltpu.CompilerParams(
            dimension_semantics=("arbitrary",),
            vmem_limit_bytes=VMEM_LIMIT_BYTES),
        name="block",
    )(sinks, x.reshape(B * S, D), *const_args)
    return out.reshape(B, S, D)


def kernel(x, norm1_g, w_in, gate_b, q_norm_g, k_norm_g, sinks, w_attn_out, conv_w, conv_b,
           conv_norm_g, conv_norm_b, w_conv_out, w_out, norm2_g, w_up, w_down):
    B, S, D = x.shape
    depth = w_in.shape[0]
    assert D == D_MODEL and S % SEQ_TILE == 0
    c0, c1 = QKV_W, QKV_W + 2 * D_MODEL
    for l in range(depth):
        wqkvt = w_in[l, :, :c0].T.astype(_BF16)
        wc = w_in[l, :, c0:c1].astype(_BF16)
        wg = w_in[l, :, c1:].astype(_BF16)
        qkg = jnp.stack([
            jnp.broadcast_to((q_norm_g[l] * (HEAD_DIM ** -0.5 * LOG2_E))[:, None], (HEAD_DIM, SEQ_TILE)),
            jnp.broadcast_to(k_norm_g[l][:, None], (HEAD_DIM, SEQ_TILE))])
        x = _block_call(
            x, sinks[l], norm1_g[l][None], wqkvt, qkg, wc, wg, gate_b[l][None],
            w_attn_out[l].astype(_BF16),
            jnp.broadcast_to(conv_w[l][:, None, :], (CONV_WIDTH, SUBLANES, D_MODEL)),
            conv_b[l][None], conv_norm_g[l][None],
            conv_norm_b[l][None], w_conv_out[l].astype(_BF16), w_out[l].astype(_BF16),
            norm2_g[l][None], w_up[l].astype(_BF16), w_down[l].astype(_BF16))
    return x
```
